```python
import jax
import jax.numpy as jnp
from jax import lax
import numpy as np

D_MODEL = 2048
BATCH = 8
SEQ = 4096
DEPTH = 1

GRID_W = 64
CTX_LEN = 256
MIX_WIDTH = D_MODEL
RWKV_WIDTH = D_MODEL // 2
RWKV_HEAD = 64
RWKV_HEADS = RWKV_WIDTH // RWKV_HEAD
W_LORA = 64
A_LORA = 64
G_LORA = 128
GMLP_WIDTH = MIX_WIDTH - RWKV_WIDTH
GMLP_GROUPS = 16
GMLP_GROUP = GMLP_WIDTH // GMLP_GROUPS
CHUNK = 128
D_FF = 5632
N_MOD = 9
RWKV_IN = 3 * RWKV_WIDTH + W_LORA + A_LORA + G_LORA
IN_WIDTH = RWKV_IN + 2 * GMLP_WIDTH
RWKV_SPLITS = (RWKV_WIDTH, 2 * RWKV_WIDTH, 3 * RWKV_WIDTH,
               3 * RWKV_WIDTH + W_LORA, 3 * RWKV_WIDTH + W_LORA + A_LORA)
ALPHA = (2.0 * DEPTH) ** 0.25
BETA = (8.0 * DEPTH) ** -0.25
LN_EPS = 1e-5
GN_EPS = 64e-5

kernel_name = "hymba_rwkv7_gmlp_macaron_deepnorm_dit"


def layer_norm(x, g, b, eps=LN_EPS):
    xf = x.astype(jnp.float32)
    mu = jnp.mean(xf, axis=-1, keepdims=True)
    var = jnp.mean(jnp.square(xf - mu), axis=-1, keepdims=True)
    return ((xf - mu) * lax.rsqrt(var + eps) * g + b).astype(x.dtype)


def modulate(h, mod, j):
    return h * (1.0 + mod[3 * j + 1]) + mod[3 * j], mod[3 * j + 2]


def swiglu(h, wi, wo):
    gate, up = jnp.split(h @ wi, 2, axis=-1)
    return (jax.nn.silu(gate) * up) @ wo


def heads(t):
    return t.reshape(t.shape[0], t.shape[1], RWKV_HEADS, RWKV_HEAD)


def grid_shift(p, rows):
    b, s, ch = p.shape
    q = p.reshape(b, rows, GRID_W, ch // 4, 4)
    zc = jnp.zeros_like(q[:, :, :1, :, 0])
    zr = jnp.zeros_like(q[:, :1, :, :, 0])
    left = jnp.concatenate([zc, q[:, :, :-1, :, 0]], axis=2)
    right = jnp.concatenate([q[:, :, 1:, :, 1], zc], axis=2)
    up = jnp.concatenate([zr, q[:, :-1, :, :, 2]], axis=1)
    down = jnp.concatenate([q[:, 1:, :, :, 3], zr], axis=1)
    return jnp.stack([left, right, up, down], axis=-1).reshape(b, s, ch)


def seq_shift(p):
    prev = jnp.pad(p, ((0, 0), (1, 0), (0, 0)))[:, :-1]
    nxt = jnp.pad(p, ((0, 0), (0, 1), (0, 0)))[:, 1:]
    even = (jnp.arange(p.shape[-1]) % 2) == 0
    return jnp.where(even, prev, nxt)


def rwkv_prepare(rw, shifted, mu, w0, w_up, a0, a_up, g_up, k_k, k_a):
    xs = rw + (shifted - rw) * mu
    r, k, v, w_lo, a_lo, g_lo = jnp.split(xs, RWKV_SPLITS, axis=-1)
    g = jax.nn.sigmoid(g_lo) @ g_up
    kk = heads(k * k_k).astype(jnp.float32)
    kk = kk / jnp.maximum(jnp.sqrt(jnp.sum(kk * kk, axis=-1, keepdims=True)), 1e-12)
    dirs = []
    for d in range(2):
        w = -jax.nn.softplus(-(w0[d] + jnp.tanh(w_lo) @ w_up[d])) - 0.5
        iclr = jax.nn.sigmoid(a0[d] + a_lo @ a_up[d])
        k_d = k * (1.0 + (iclr - 1.0) * k_a)
        decay = heads(jnp.exp(-jnp.exp(w.astype(jnp.float32))))
        dirs.append((decay, heads(k_d), -kk, kk * heads(iclr).astype(jnp.float32)))
    return heads(r), heads(v), g, dirs


def wkv_scan(r, v, decay, k, a, b, s0, reverse):
    def step(s, inp):
        r_t, v_t, w_t, k_t, a_t, b_t = inp
        sa = jnp.einsum('bhvk,bhk->bhv', s, a_t)
        s = (s * w_t[:, :, None, :] + sa[..., None] * b_t[:, :, None, :]
             + v_t[..., None] * k_t[:, :, None, :])
        return s, jnp.einsum('bhvk,bhk->bhv', s, r_t)
    xs = tuple(jnp.swapaxes(t.astype(jnp.float32), 0, 1) for t in (r, v, decay, k, a, b))
    s_fin, ys = lax.scan(step, s0, xs, reverse=reverse)
    return s_fin, jnp.swapaxes(ys, 0, 1)


def rwkv_output(r, v, g, dirs, ys, r_k, gn_g, gn_b, dtype):
    y = ys[0] + ys[1]
    mu = jnp.mean(y, axis=-1, keepdims=True)
    var = jnp.mean(jnp.square(y - mu), axis=-1, keepdims=True)
    bsz, t = y.shape[0], y.shape[1]
    yn = ((y - mu) * lax.rsqrt(var + GN_EPS)).reshape(bsz, t, RWKV_WIDTH) * gn_g + gn_b
    rf, vf = r.astype(jnp.float32), v.astype(jnp.float32)
    bonus = sum(jnp.sum(rf * dd[1].astype(jnp.float32) * r_k, axis=-1, keepdims=True) * vf
                for dd in dirs)
    out = (yn + bonus.reshape(bsz, t, RWKV_WIDTH)) * g.astype(jnp.float32)
    return out.astype(dtype)


def gmlp_mix(gm, ln_g, ln_b, ws, bs):
    bsz, t, _ = gm.shape
    u, v = jnp.split(jax.nn.gelu(gm), 2, axis=-1)
    v = v.reshape(bsz, t // CHUNK, CHUNK, GMLP_GROUPS, GMLP_GROUP)
    v = layer_norm(v, ln_g.reshape(GMLP_GROUPS, GMLP_GROUP), ln_b.reshape(GMLP_GROUPS, GMLP_GROUP))
    mixed = jnp.einsum('gpq,bnqgd->bnpgd', ws, v) + bs.T[None, None, :, :, None]
    return u * mixed.reshape(bsz, t, GMLP_WIDTH)


def token_mixer(h_x, h_c, rows, with_ctx_out, w_in, mu_shift, w0, w_up, a0, a_up, g_up,
                k_k, k_a, r_k, gn_g, gn_b, gm_ln_g, gm_ln_b, gm_ws, gm_bs, w_out):
    p_x = h_x @ w_in
    p_c = h_c @ w_in
    rw_x, gm_x = p_x[..., :RWKV_IN], p_x[..., RWKV_IN:]
    rw_c, gm_c = p_c[..., :RWKV_IN], p_c[..., RWKV_IN:]
    rparams = (mu_shift, w0, w_up, a0, a_up, g_up, k_k, k_a)
    r_c, v_c, g_c, dirs_c = rwkv_prepare(rw_c, seq_shift(rw_c), *rparams)
    r_x, v_x, g_x, dirs_x = rwkv_prepare(rw_x, grid_shift(rw_x, rows), *rparams)
    s_zero = jnp.zeros((h_x.shape[0], RWKV_HEADS, RWKV_HEAD, RWKV_HEAD), jnp.float32)
    ys_x, ys_c = [], []
    for d in range(2):
        s_c, y_c = wkv_scan(r_c, v_c, *dirs_c[d], s_zero, reverse=(d == 1))
        _, y_x = wkv_scan(r_x, v_x, *dirs_x[d], s_c, reverse=(d == 1))
        ys_c.append(y_c)
        ys_x.append(y_x)
    out_x = jnp.concatenate([
        rwkv_output(r_x, v_x, g_x, dirs_x, ys_x, r_k, gn_g, gn_b, h_x.dtype),
        gmlp_mix(gm_x, gm_ln_g, gm_ln_b, gm_ws, gm_bs)], axis=-1) @ w_out
    out_c = None
    if with_ctx_out:
        out_c = jnp.concatenate([
            rwkv_output(r_c, v_c, g_c, dirs_c, ys_c, r_k, gn_g, gn_b, h_c.dtype),
            gmlp_mix(gm_c, gm_ln_g, gm_ln_b, gm_ws, gm_bs)], axis=-1) @ w_out
    return out_x, out_c


def setup_inputs(seed: int = 0) -> dict:
    key = jax.random.key(seed)
    ks = jax.random.split(key, 32)
    f32 = jnp.float32
    L, D = DEPTH, D_MODEL

    def nrm(k, shape, scale):
        return jax.random.normal(k, shape, f32) * scale

    w0_base = jnp.linspace(-6.0, -1.0, RWKV_WIDTH, dtype=f32)
    return {
        "x": nrm(ks[0], (BATCH, SEQ, D), 1.0),
        "c": nrm(ks[1], (BATCH, D), 1.0),
        "ctx": nrm(ks[2], (BATCH, CTX_LEN, D), 1.0),
        "c_ctx": nrm(ks[3], (D,), 1.0),
        "w_ada": nrm(ks[4], (L, D, N_MOD * D), 0.5 * D ** -0.5),
        "b_ada": nrm(ks[5], (L, N_MOD * D), 0.02),
        "ln_g": 1.0 + nrm(ks[6], (L, 3, D), 0.02),
        "ln_b": nrm(ks[7], (L, 3, D), 0.02),
        "ffn_a_wi": nrm(ks[8], (L, D, 2 * D_FF), D ** -0.5),
        "ffn_a_wo": nrm(ks[9], (L, D_FF, D), BETA * D_FF ** -0.5),
        "ffn_b_wi": nrm(ks[10], (L, D, 2 * D_FF), D ** -0.5),
        "ffn_b_wo": nrm(ks[11], (L, D_FF, D), BETA * D_FF ** -0.5),
        "w_in": nrm(ks[12], (L, D, IN_WIDTH), D ** -0.5),
        "mu_shift": jax.random.uniform(ks[13], (L, RWKV_IN), f32),
        "w0": w0_base + nrm(ks[14], (L, 2, RWKV_WIDTH), 0.1),
        "w_up": nrm(ks[15], (L, 2, W_LORA, RWKV_WIDTH), 0.1 * W_LORA ** -0.5),
        "a0": nrm(ks[16], (L, 2, RWKV_WIDTH), 0.1),
        "a_up": nrm(ks[17], (L, 2, A_LORA, RWKV_WIDTH), A_LORA ** -0.5),
        "g_up": nrm(ks[18], (L, G_LORA, RWKV_WIDTH), G_LORA ** -0.5),
        "k_k": 0.85 + nrm(ks[19], (L, RWKV_WIDTH), 0.02),
        "k_a": 1.0 + nrm(ks[20], (L, RWKV_WIDTH), 0.02),
        "r_k": nrm(ks[21], (L, RWKV_HEADS, RWKV_HEAD), 0.1),
        "gn_g": 1.0 + nrm(ks[22], (L, RWKV_WIDTH), 0.02),
        "gn_b": nrm(ks[23], (L, RWKV_WIDTH), 0.02),
        "gm_ln_g": 1.0 + nrm(ks[24], (L, GMLP_WIDTH), 0.02),
        "gm_ln_b": nrm(ks[25], (L, GMLP_WIDTH), 0.02),
        "gm_ws": nrm(ks[26], (L, GMLP_GROUPS, CHUNK, CHUNK), CHUNK ** -0.5),
        "gm_bs": 1.0 + nrm(ks[27], (L, GMLP_GROUPS, CHUNK), 0.02),
        "w_out": nrm(ks[28], (L, MIX_WIDTH, D), BETA * MIX_WIDTH ** -0.5),
    }


def reference(x, c, ctx, c_ctx, w_ada, b_ada, ln_g, ln_b, ffn_a_wi, ffn_a_wo, ffn_b_wi,
              ffn_b_wo, w_in, mu_shift, w0, w_up, a0, a_up, g_up, k_k, k_a, r_k, gn_g, gn_b,
              gm_ln_g, gm_ln_b, gm_ws, gm_bs, w_out):
    bsz, seq_len, d = x.shape
    rows = seq_len // GRID_W
    cx = ctx
    for i in range(DEPTH):
        last = i == DEPTH - 1
        mod_x = (jax.nn.silu(c) @ w_ada[i] + b_ada[i]).reshape(bsz, N_MOD, d).transpose(1, 0, 2)[:, :, None, :]
        mod_c = (jax.nn.silu(c_ctx) @ w_ada[i] + b_ada[i]).reshape(N_MOD, 1, 1, d)

        hx, gx = modulate(x, mod_x, 0)
        hc, gc = modulate(cx, mod_c, 0)
        x = layer_norm(ALPHA * x + 0.5 * gx * swiglu(hx, ffn_a_wi[i], ffn_a_wo[i]), ln_g[i, 0], ln_b[i, 0])
        cx = layer_norm(ALPHA * cx + 0.5 * gc * swiglu(hc, ffn_a_wi[i], ffn_a_wo[i]), ln_g[i, 0], ln_b[i, 0])

        hx, gx = modulate(x, mod_x, 1)
        hc, gc = modulate(cx, mod_c, 1)
        out_x, out_c = token_mixer(hx, hc, rows, not last, w_in[i], mu_shift[i], w0[i], w_up[i],
                                   a0[i], a_up[i], g_up[i], k_k[i], k_a[i], r_k[i], gn_g[i], gn_b[i],
                                   gm_ln_g[i], gm_ln_b[i], gm_ws[i], gm_bs[i], w_out[i])
        x = layer_norm(ALPHA * x + gx * out_x, ln_g[i, 1], ln_b[i, 1])
        if not last:
            cx = layer_norm(ALPHA * cx + gc * out_c, ln_g[i, 1], ln_b[i, 1])

        hx, gx = modulate(x, mod_x, 2)
        x = layer_norm(ALPHA * x + 0.5 * gx * swiglu(hx, ffn_b_wi[i], ffn_b_wo[i]), ln_g[i, 2], ln_b[i, 2])
        if not last:
            hc, gc = modulate(cx, mod_c, 2)
            cx = layer_norm(ALPHA * cx + 0.5 * gc * swiglu(hc, ffn_b_wi[i], ffn_b_wo[i]), ln_g[i, 2], ln_b[i, 2])
    return x
```

```python
import functools
import math

import jax
import jax.numpy as jnp
from jax.experimental import pallas as pl
from jax.experimental.pallas import tpu as pltpu

F32 = jnp.float32
BF16 = jnp.bfloat16

GRID_W = 64
RWKV_HEAD = 64
CHUNK = 128
GMLP_GROUP = 64
N_MOD = 9
LN_EPS = 1e-5
GN_EPS = 64e-5
DEPTH = 1
ALPHA = (2.0 * DEPTH) ** 0.25

SCAN_CHUNK = 64
HEADS_PER_GROUP = 4
GROUP_LANES = HEADS_PER_GROUP * RWKV_HEAD
VMEM_LIMIT = 56 * 1024 * 1024


def _cparams(sem):
    return pltpu.CompilerParams(dimension_semantics=sem, vmem_limit_bytes=VMEM_LIMIT)


def _pick(n, pref):
    t = min(n, pref)
    while n % t:
        t -= 64
    return t


def _sigmoid(z):
    return 1.0 / (1.0 + jnp.exp(-z))


def _silu(z):
    return z * _sigmoid(z)


def _gelu_tanh(z):
    return 0.5 * z * (1.0 + jnp.tanh(math.sqrt(2.0 / math.pi) * (z + 0.044715 * (z * z * z))))


def _layer_norm(z, g, b):
    mu = jnp.mean(z, axis=-1, keepdims=True)
    d = z - mu
    var = jnp.mean(d * d, axis=-1, keepdims=True)
    return d * jax.lax.rsqrt(var + LN_EPS) * g + b


def _dot(a, b):
    return jnp.dot(a, b, preferred_element_type=F32)


def _dot_nt(a, b):
    return jax.lax.dot_general(a, b, (((1,), (1,)), ((), ())), preferred_element_type=F32)


def _split2(z):
    hi = z.astype(BF16)
    lo = (z - hi.astype(F32)).astype(BF16)
    return hi, lo


def _group_sum(z, bd):
    hi, lo = _split2(z)
    return _dot(hi, bd) + _dot(lo, bd)


def _head_sum(z, bd):
    w = z.shape[-1]
    return jnp.concatenate([_group_sum(z[:, s:s + GROUP_LANES], bd) for s in range(0, w, GROUP_LANES)], axis=1)


def _ada_kernel(c_ref, w_ref, b_ref, o_ref):
    a = _silu(c_ref[...]).astype(BF16)
    o_ref[...] = _dot(a, w_ref[...].astype(BF16)) + b_ref[...]


def _ada(cc, w_ada, b_ada):
    m, d = cc.shape
    n = w_ada.shape[1]
    tn = 1024
    return pl.pallas_call(
        _ada_kernel,
        grid=(n // tn,),
        in_specs=[pl.BlockSpec((m, d), lambda j: (0, 0)),
                  pl.BlockSpec((d, tn), lambda j: (0, j)),
                  pl.BlockSpec((1, tn), lambda j: (0, j))],
        out_specs=pl.BlockSpec((m, tn), lambda j: (0, j)),
        out_shape=jax.ShapeDtypeStruct((m, n), F32),
        compiler_params=_cparams(("arbitrary",)),
        name="ada",
    )(cc, w_ada, b_ada.reshape(1, n))


def _ffn_kernel(x_ref, mod_ref, wig_ref, wiu_ref, wo_ref, g_ref, b_ref, o_ref, h_sc, acc_sc, *, slot):
    f = pl.program_id(1)

    @pl.when(f == 0)
    def _():
        shift = mod_ref[0, 3 * slot:3 * slot + 1, :]
        scale = mod_ref[0, 3 * slot + 1:3 * slot + 2, :]
        h_sc[...] = (x_ref[...] * (1.0 + scale) + shift).astype(BF16)
        acc_sc[...] = jnp.zeros_like(acc_sc)

    h = h_sc[...]
    gate = _dot(h, wig_ref[...])
    up = _dot(h, wiu_ref[...])
    act = (_silu(gate) * up).astype(BF16)
    acc_sc[...] += _dot(act, wo_ref[...])

    @pl.when(f == pl.num_programs(1) - 1)
    def _():
        gmod = mod_ref[0, 3 * slot + 2:3 * slot + 3, :]
        z = ALPHA * x_ref[...] + 0.5 * gmod * acc_sc[...]
        o_ref[...] = _layer_norm(z, g_ref[...], b_ref[...])


def _ffn(x2, mods, mod_row, wi, wo, ln_g, ln_b, slot, tm):
    ntok, d = x2.shape
    dff = wo.shape[0]
    tf = 512
    nf = dff // tf
    return pl.pallas_call(
        functools.partial(_ffn_kernel, slot=slot),
        grid=(ntok // tm, nf),
        in_specs=[pl.BlockSpec((tm, d), lambda i, f: (i, 0)),
                  pl.BlockSpec((1, N_MOD, d), lambda i, f: (mod_row(i), 0, 0)),
                  pl.BlockSpec((d, tf), lambda i, f: (0, f)),
                  pl.BlockSpec((d, tf), lambda i, f: (0, nf + f)),
                  pl.BlockSpec((tf, d), lambda i, f: (f, 0)),
                  pl.BlockSpec((1, d), lambda i, f: (0, 0)),
                  pl.BlockSpec((1, d), lambda i, f: (0, 0))],
        out_specs=pl.BlockSpec((tm, d), lambda i, f: (i, 0)),
        out_shape=jax.ShapeDtypeStruct((ntok, d), F32),
        scratch_shapes=[pltpu.VMEM((tm, d), BF16), pltpu.VMEM((tm, d), F32)],
        compiler_params=_cparams(("parallel", "arbitrary")),
        name=f"ffn{slot}",
    )(x2, mods, wi, wi, wo, ln_g.reshape(1, d), ln_b.reshape(1, d))


def _inproj_kernel(x_ref, mod_ref, w_ref, o_ref, *, slot):
    shift = mod_ref[0, 3 * slot:3 * slot + 1, :]
    scale = mod_ref[0, 3 * slot + 1:3 * slot + 2, :]
    h = (x_ref[...] * (1.0 + scale) + shift).astype(BF16)
    o_ref[...] = _dot(h, w_ref[...])


def _inproj(x2, mods, mod_row, w, slot, tm, tn):
    ntok, d = x2.shape
    n = w.shape[1]
    return pl.pallas_call(
        functools.partial(_inproj_kernel, slot=slot),
        grid=(n // tn, ntok // tm),
        in_specs=[pl.BlockSpec((tm, d), lambda j, i: (i, 0)),
                  pl.BlockSpec((1, N_MOD, d), lambda j, i: (mod_row(i), 0, 0)),
                  pl.BlockSpec((d, tn), lambda j, i: (0, j))],
        out_specs=pl.BlockSpec((tm, tn), lambda j, i: (i, j)),
        out_shape=jax.ShapeDtypeStruct((ntok, n), F32),
        compiler_params=_cparams(("parallel", "parallel")),
        name="inproj",
    )(x2, mods, w)


def _prep_kernel(*refs, grid_mode, tiles_per_seq, rw):
    if grid_mode:
        p_ref, up_ref, dn_ref = refs[:3]
        refs = refs[3:]
    else:
        p_ref = refs[0]
        refs = refs[1:]
    (mu_ref, w0_ref, wup_ref, a0_ref, aup_ref, gup_ref, kk_ref, ka_ref, bd_ref,
     r_o, v_o, g_o, kk_o, lw0_o, lw1_o, kd0_o, kd1_o, ic0_o, ic1_o) = refs
    p = p_ref[...]
    tm, width = p.shape
    row = jax.lax.broadcasted_iota(jnp.int32, (tm, width), 0)
    lane = jax.lax.broadcasted_iota(jnp.int32, (tm, width), 1)
    prev = pltpu.roll(p, 1, axis=0)
    nxt = pltpu.roll(p, tm - 1, axis=0)
    if grid_mode:
        i = pl.program_id(0) % tiles_per_seq
        col = row % GRID_W
        left = jnp.where(col > 0, prev, 0.0)
        right = jnp.where(col < GRID_W - 1, nxt, 0.0)
        up_halo = jnp.where(i > 0, up_ref[...], 0.0)
        dn_halo = jnp.where(i < tiles_per_seq - 1, dn_ref[...], 0.0)
        up = jnp.concatenate([up_halo, p[:tm - GRID_W]], axis=0)
        down = jnp.concatenate([p[GRID_W:], dn_halo], axis=0)
        c4 = lane % 4
        shifted = jnp.where(c4 == 0, left, jnp.where(c4 == 1, right, jnp.where(c4 == 2, up, down)))
    else:
        prev = jnp.where(row > 0, prev, 0.0)
        nxt = jnp.where(row < tm - 1, nxt, 0.0)
        shifted = jnp.where(lane % 2 == 0, prev, nxt)
    xs = p + (shifted - p) * mu_ref[...]
    r = xs[:, 0:rw]
    k = xs[:, rw:2 * rw]
    v = xs[:, 2 * rw:3 * rw]
    wa_lo = xs[:, 3 * rw:3 * rw + 128]
    g_lo = xs[:, 3 * rw + 128:3 * rw + 256]
    tanh_wa = jnp.tanh(wa_lo).astype(BF16)
    wa_bf = wa_lo.astype(BF16)
    r_o[...] = r
    v_o[...] = v
    g_o[...] = _dot(_sigmoid(g_lo).astype(BF16), gup_ref[...])
    kk = k * kk_ref[...]
    ss = _head_sum(kk * kk, bd_ref[...])
    kk_o[...] = kk / jnp.maximum(jnp.sqrt(ss), 1e-12)
    for d, (lw_o, kd_o, ic_o) in enumerate(((lw0_o, kd0_o, ic0_o), (lw1_o, kd1_o, ic1_o))):
        z = w0_ref[d:d + 1, :] + _dot(tanh_wa, wup_ref[d])
        lw_o[...] = -math.exp(-0.5) * _sigmoid(z)
        iclr = _sigmoid(a0_ref[d:d + 1, :] + _dot(wa_bf, aup_ref[d]))
        ic_o[...] = iclr
        kd_o[...] = k * (1.0 + (iclr - 1.0) * ka_ref[...])


def _prep(p_rw, params, bd, grid_mode, tokens_per_seq, tm):
    ntok, width = p_rw.shape
    rw = params["w0"].shape[1]
    tiles_per_seq = tokens_per_seq // tm
    hb = tm // GRID_W
    nhalo = ntok // GRID_W
    const = lambda shape: pl.BlockSpec(shape, lambda i: (0,) * len(shape))
    in_specs = [pl.BlockSpec((tm, width), lambda i: (i, 0))]
    args = [p_rw]
    if grid_mode:
        in_specs += [pl.BlockSpec((GRID_W, width), lambda i: (jnp.maximum(i * hb - 1, 0), 0)),
                     pl.BlockSpec((GRID_W, width), lambda i: (jnp.minimum((i + 1) * hb, nhalo - 1), 0))]
        args += [p_rw, p_rw]
    plist = [params[k] for k in ("mu", "w0", "w_up", "a0", "a_up", "g_up", "k_k", "k_a")] + [bd]
    in_specs += [const(a.shape) for a in plist]
    out_spec = pl.BlockSpec((tm, rw), lambda i: (i, 0))
    return pl.pallas_call(
        functools.partial(_prep_kernel, grid_mode=grid_mode, tiles_per_seq=tiles_per_seq, rw=rw),
        grid=(ntok // tm,),
        in_specs=in_specs,
        out_specs=[out_spec] * 10,
        out_shape=[jax.ShapeDtypeStruct((ntok, rw), F32)] * 10,
        compiler_params=_cparams(("parallel",)),
        name="prep_grid" if grid_mode else "prep_seq",
    )(*args, *plist)


def _stackmask(z, lane_head):
    return jnp.concatenate([jnp.where(lane_head == h, z, jnp.zeros_like(z)) for h in range(HEADS_PER_GROUP)],
                           axis=0)


def _scan_unit(s, r, v, kd, b, kk, dinc, dexc, dinv, dend, dtot, reverse, with_y):
    c, l = r.shape
    gc = HEADS_PER_GROUP * c
    t = jax.lax.broadcasted_iota(jnp.int32, (c, gc), 0)
    tj = jax.lax.broadcasted_iota(jnp.int32, (c, gc), 1) % c
    lane_head = jax.lax.broadcasted_iota(jnp.int32, (c, l), 1) // RWKV_HEAD
    strict = (tj > t) if reverse else (tj < t)
    incl = (tj >= t) if reverse else (tj <= t)

    at = (-kk * dexc).astype(BF16)
    rt = (r * dinc).astype(BF16)
    bt = (b * dinv).astype(BF16)
    kt = (kd * dinv).astype(BF16)
    lhs = jnp.concatenate([at, rt], axis=0)
    rhs = jnp.concatenate([_stackmask(bt, lane_head), _stackmask(kt, lane_head)], axis=0)
    a_all = _dot_nt(lhs, rhs)
    a_ab = jnp.where(strict, a_all[:c, :gc], 0.0)
    a_ak = jnp.where(strict, a_all[:c, gc:], 0.0).astype(BF16)

    p = jnp.where(tj == t, 1.0, 0.0) + a_ab
    apow = a_ab.astype(BF16)
    n_sq = int(math.log2(c)) - 1
    for _ in range(n_sq):
        apow = _dot(apow, _stackmask(apow, lane_head)).astype(BF16)
        p = p + _dot(p.astype(BF16), _stackmask(apow, lane_head))

    s_bf = s.astype(BF16)
    v_bf = v.astype(BF16)
    vbd = _stackmask(v_bf, lane_head)
    u = _dot_nt(at, s_bf) + _dot(a_ak, vbd)
    sa = _dot(p.astype(BF16), _stackmask(u.astype(BF16), lane_head))
    y = None
    if with_y:
        a_r = jnp.concatenate([jnp.where(incl, a_all[c:, :gc], 0.0), jnp.where(incl, a_all[c:, gc:], 0.0)],
                              axis=1).astype(BF16)
        y = _dot_nt(rt, s_bf) + _dot(a_r, jnp.concatenate([_stackmask(sa.astype(BF16), lane_head), vbd], axis=0))
    sav_t = jnp.concatenate([sa, v], axis=0).T.astype(BF16)
    bk = jnp.concatenate([b * dend, kd * dend], axis=0).astype(BF16)
    upd = _dot(sav_t, bk)
    rh = jax.lax.broadcasted_iota(jnp.int32, (l, l), 0) // RWKV_HEAD
    ch = jax.lax.broadcasted_iota(jnp.int32, (l, l), 1) // RWKV_HEAD
    s_new = s * dtot + jnp.where(rh == ch, upd, 0.0)
    return s_new, y


def _scan_kernel(*refs, with_y):
    ins = refs[:13]
    s0_ref = ins[12]
    if with_y:
        yf_ref, yb_ref, sfin_ref, s_sc = refs[13:]
        y_refs = (yf_ref, yb_ref)
    else:
        sfin_ref, s_sc = refs[13:]
        y_refs = (None, None)
    ci = pl.program_id(1)

    @pl.when(ci == 0)
    def _():
        s_sc[...] = s0_ref[0]

    for d in range(2):
        r_ref, v_ref, kk_ref, lw_ref, kd_ref, ic_ref = ins[6 * d:6 * d + 6]
        reverse = d == 1
        lw = lw_ref[0]
        c, width = lw.shape
        ti = jax.lax.broadcasted_iota(jnp.int32, (c, c), 0)
        tj = jax.lax.broadcasted_iota(jnp.int32, (c, c), 1)
        tri = jnp.where((tj >= ti) if reverse else (tj <= ti), 1.0, 0.0).astype(BF16)
        hi = lw.astype(BF16)
        rem = lw - hi.astype(F32)
        mid = rem.astype(BF16)
        lo = (rem - mid.astype(F32)).astype(BF16)
        cum = _dot(tri, hi) + _dot(tri, mid) + _dot(tri, lo)
        tot = cum[0:1, :] if reverse else cum[c - 1:c, :]
        dinc = jnp.exp(cum)
        dexc = jnp.exp(cum - lw)
        dinv = jnp.exp(-cum)
        dend = jnp.exp(tot - cum)
        dtot = jnp.exp(tot)
        r = r_ref[0]
        v = v_ref[0]
        kk = kk_ref[0]
        kd = kd_ref[0]
        b = kk * ic_ref[0]
        for g in range(width // GROUP_LANES):
            sl = slice(g * GROUP_LANES, (g + 1) * GROUP_LANES)
            s_new, y = _scan_unit(s_sc[d, g], r[:, sl], v[:, sl], kd[:, sl], b[:, sl], kk[:, sl],
                                  dinc[:, sl], dexc[:, sl], dinv[:, sl], dend[:, sl], dtot[:, sl],
                                  reverse, with_y)
            s_sc[d, g] = s_new
            if with_y:
                y_refs[d][0, :, sl] = y

    @pl.when(ci == pl.num_programs(1) - 1)
    def _():
        sfin_ref[0] = s_sc[...]


def _scan(q, s0, with_y):
    bsz, t, rw = q["r"].shape
    c = SCAN_CHUNK
    nc = t // c
    ng = rw // GROUP_LANES
    fwd = pl.BlockSpec((1, c, rw), lambda b, i: (b, i, 0))
    bwd = pl.BlockSpec((1, c, rw), lambda b, i: (b, nc - 1 - i, 0))
    s_spec = pl.BlockSpec((1, 2, ng, GROUP_LANES, GROUP_LANES), lambda b, i: (b, 0, 0, 0, 0))
    args = [q["r"], q["v"], q["kk"], q["lw0"], q["kd0"], q["ic0"],
            q["r"], q["v"], q["kk"], q["lw1"], q["kd1"], q["ic1"], s0]
    in_specs = [fwd] * 6 + [bwd] * 6 + [s_spec]
    y_shape = jax.ShapeDtypeStruct((bsz, t, rw), F32)
    out_specs = ([fwd, bwd] if with_y else []) + [s_spec]
    out_shape = ([y_shape, y_shape] if with_y else []) + [jax.ShapeDtypeStruct(s0.shape, F32)]
    outs = pl.pallas_call(
        functools.partial(_scan_kernel, with_y=with_y),
        grid=(bsz, nc),
        in_specs=in_specs,
        out_specs=out_specs,
        out_shape=out_shape,
        scratch_shapes=[pltpu.VMEM((2, ng, GROUP_LANES, GROUP_LANES), F32)],
        compiler_params=_cparams(("parallel", "arbitrary")),
        name="scan_y" if with_y else "scan_state",
    )(*args)
    return outs


def _mixout_kernel(yf_ref, yb_ref, r_ref, v_ref, g_ref, kd0_ref, kd1_ref, gm_ref, x_ref, mod_ref,
                   rk_ref, gng_ref, gnb_ref, glg_ref, glb_ref, ws_ref, bs_ref, wout_ref, lng_ref, lnb_ref,
                   bd_ref, o_ref, *, slot):
    bd = bd_ref[...]
    rw = yf_ref.shape[-1]
    inv_n = 1.0 / RWKV_HEAD
    y = yf_ref[...] + yb_ref[...]
    mu = _head_sum(y, bd) * inv_n
    dy = y - mu
    var = _head_sum(dy * dy, bd) * inv_n
    yn = dy * jax.lax.rsqrt(var + GN_EPS) * gng_ref[...] + gnb_ref[...]
    r = r_ref[...]
    bonus = _head_sum(r * (kd0_ref[...] + kd1_ref[...]) * rk_ref[...], bd) * v_ref[...]
    out_r = ((yn + bonus) * g_ref[...]).astype(BF16)

    gm = _gelu_tanh(gm_ref[...])
    u = gm[:, :rw]
    vv = gm[:, rw:]
    inv_g = 1.0 / GMLP_GROUP
    mu_v = _head_sum(vv, bd) * inv_g
    dv = vv - mu_v
    var_v = _head_sum(dv * dv, bd) * inv_g
    vn = (dv * jax.lax.rsqrt(var_v + LN_EPS) * glg_ref[...] + glb_ref[...]).astype(BF16)
    tm = vn.shape[0]
    lane = jax.lax.broadcasted_iota(jnp.int32, (CHUNK, 2 * GMLP_GROUP), 1)
    chunks = []
    for n in range(tm // CHUNK):
        pairs = []
        for gp in range(rw // (2 * GMLP_GROUP)):
            v2 = vn[n * CHUNK:(n + 1) * CHUNK, gp * 2 * GMLP_GROUP:(gp + 1) * 2 * GMLP_GROUP]
            m0 = _dot(ws_ref[2 * gp], v2)
            m1 = _dot(ws_ref[2 * gp + 1], v2)
            pairs.append(jnp.where(lane < GMLP_GROUP, m0, m1))
        chunks.append(jnp.concatenate(pairs, axis=1) + bs_ref[...])
    mixed = jnp.concatenate(chunks, axis=0)
    out_g = (u * mixed).astype(BF16)

    o = _dot(out_r, wout_ref[0:rw, :]) + _dot(out_g, wout_ref[rw:, :])
    gate = mod_ref[0, 3 * slot + 2:3 * slot + 3, :]
    z = ALPHA * x_ref[...] + gate * o
    o_ref[...] = _layer_norm(z, lng_ref[...], lnb_ref[...])


def _mixout(yf, yb, q, p_gm, x2, mods, mod_row, params, bd, slot, tm):
    ntok, d = x2.shape
    rw = yf.shape[-1]
    tok = lambda w: pl.BlockSpec((tm, w), lambda i: (i, 0))
    const = lambda a: pl.BlockSpec(a.shape, lambda i: (0,) * a.ndim)
    plist = [params[k] for k in ("r_k", "gn_g", "gn_b", "gm_ln_g", "gm_ln_b", "gm_ws", "gm_bs_full", "w_out",
                                 "ln_g", "ln_b")] + [bd]
    return pl.pallas_call(
        functools.partial(_mixout_kernel, slot=slot),
        grid=(ntok // tm,),
        in_specs=[tok(rw)] * 7 + [tok(p_gm.shape[1]), tok(d),
                                  pl.BlockSpec((1, N_MOD, d), lambda i: (mod_row(i), 0, 0))]
                 + [const(a) for a in plist],
        out_specs=tok(d),
        out_shape=jax.ShapeDtypeStruct((ntok, d), F32),
        compiler_params=_cparams(("parallel",)),
        name="mixout",
    )(yf, yb, q["r"], q["v"], q["g"], q["kd0"], q["kd1"], p_gm, x2, mods, *plist)


_PREP_NAMES = ("r", "v", "g", "kk", "lw0", "lw1", "kd0", "kd1", "ic0", "ic1")


def kernel(x, c, ctx, c_ctx, w_ada, b_ada, ln_g, ln_b, ffn_a_wi, ffn_a_wo, ffn_b_wi, ffn_b_wo, w_in, mu_shift,
           w0, w_up, a0, a_up, g_up, k_k, k_a, r_k, gn_g, gn_b, gm_ln_g, gm_ln_b, gm_ws, gm_bs, w_out):
    bsz, seq, d = x.shape
    ctx_len = ctx.shape[1]
    assert w_ada.shape[0] == DEPTH
    rw = w0.shape[-1]
    rwkv_in = mu_shift.shape[-1]
    w_lora, a_lora = w_up.shape[2], a_up.shape[2]
    assert w_lora + a_lora == 128 and g_up.shape[1] == 128 and rwkv_in == 3 * rw + 256
    assert seq % GRID_W == 0 and seq % SCAN_CHUNK == 0 and ctx_len % SCAN_CHUNK == 0 and seq % CHUNK == 0
    i = 0

    n_rows = -(-(bsz + 1) // 8) * 8
    cc = jnp.zeros((n_rows, d), F32).at[:bsz].set(c).at[bsz].set(c_ctx)
    mods = _ada(cc, w_ada[i], b_ada[i]).reshape(n_rows, N_MOD, d)

    x2 = x.reshape(bsz * seq, d)
    c2 = ctx.reshape(bsz * ctx_len, d)
    tm_x = _pick(seq, 512)
    tm_c = _pick(ctx_len, 512)
    row_x = lambda tm: (lambda t: t // (seq // tm))
    row_c = lambda t: bsz

    bf = lambda a: a.astype(BF16)
    wi_a, wo_a = bf(ffn_a_wi[i]), bf(ffn_a_wo[i])
    x1 = _ffn(x2, mods, row_x(tm_x), wi_a, wo_a, ln_g[i, 0], ln_b[i, 0], 0, tm_x)
    c1 = _ffn(c2, mods, row_c, wi_a, wo_a, ln_g[i, 0], ln_b[i, 0], 0, tm_c)

    w_in_bf = bf(w_in[i])
    w_rw, w_gm = w_in_bf[:, :rwkv_in], w_in_bf[:, rwkv_in:]
    p_rw_x = _inproj(x1, mods, row_x(tm_x), w_rw, 1, tm_x, rwkv_in // 2)
    p_gm_x = _inproj(x1, mods, row_x(tm_x), w_gm, 1, tm_x, w_gm.shape[1] // 2)
    p_rw_c = _inproj(c1, mods, row_c, w_rw, 1, tm_c, rwkv_in // 2)

    zpad = lambda a, lo, hi: jnp.pad(a, ((0, 0), (lo, hi), (0, 0)))
    lanes = jnp.arange(GROUP_LANES) // RWKV_HEAD
    bd = (lanes[:, None] == lanes[None, :]).astype(BF16)
    prm = dict(mu=mu_shift[i].reshape(1, rwkv_in), w0=w0[i], w_up=bf(zpad(w_up[i], 0, a_lora)), a0=a0[i],
               a_up=bf(zpad(a_up[i], w_lora, 0)), g_up=bf(g_up[i]), k_k=k_k[i].reshape(1, rw),
               k_a=k_a[i].reshape(1, rw))
    tm_p = _pick(seq, 256)
    q_x = dict(zip(_PREP_NAMES, _prep(p_rw_x, prm, bd, True, seq, tm_p)))
    q_c = dict(zip(_PREP_NAMES, _prep(p_rw_c, prm, bd, False, ctx_len, ctx_len)))
    q_x3 = {k: a.reshape(bsz, seq, rw) for k, a in q_x.items()}
    q_c3 = {k: a.reshape(bsz, ctx_len, rw) for k, a in q_c.items()}

    ng = rw // GROUP_LANES
    s_zero = jnp.zeros((bsz, 2, ng, GROUP_LANES, GROUP_LANES), F32)
    (s_ctx,) = _scan(q_c3, s_zero, with_y=False)
    yf, yb, _ = _scan(q_x3, s_ctx, with_y=True)

    out_prm = dict(r_k=r_k[i].reshape(1, rw), gn_g=gn_g[i].reshape(1, rw), gn_b=gn_b[i].reshape(1, rw),
                   gm_ln_g=gm_ln_g[i].reshape(1, rw), gm_ln_b=gm_ln_b[i].reshape(1, rw), gm_ws=bf(gm_ws[i]),
                   gm_bs_full=jnp.repeat(gm_bs[i].T, GMLP_GROUP, axis=1), w_out=bf(w_out[i]),
                   ln_g=ln_g[i, 1].reshape(1, d), ln_b=ln_b[i, 1].reshape(1, d))
    tm_o = _pick(seq, 256)
    x2b = _mixout(yf.reshape(bsz * seq, rw), yb.reshape(bsz * seq, rw), q_x, p_gm_x, x1, mods, row_x(tm_o),
                  out_prm, bd, 1, tm_o)

    x3 = _ffn(x2b, mods, row_x(tm_x), bf(ffn_b_wi[i]), bf(ffn_b_wo[i]), ln_g[i, 2], ln_b[i, 2], 2, tm_x)
    return x3.reshape(bsz, seq, d)
```

```python
import functools
import math

import jax
import jax.numpy as jnp
from jax.experimental import pallas as pl
from jax.experimental.pallas import tpu as pltpu

F32 = jnp.float32
BF16 = jnp.bfloat16

GRID_W = 64
RWKV_HEAD = 64
CHUNK = 128
GMLP_GROUP = 64
N_MOD = 9
LN_EPS = 1e-5
GN_EPS = 64e-5
DEPTH = 1
ALPHA = (2.0 * DEPTH) ** 0.25

SCAN_CHUNK = 64
HEADS_PER_GROUP = 4
GROUP_LANES = HEADS_PER_GROUP * RWKV_HEAD
VMEM_LIMIT = 56 * 1024 * 1024


def _cparams(sem):
    return pltpu.CompilerParams(dimension_semantics=sem, vmem_limit_bytes=VMEM_LIMIT)


def _pick(n, pref):
    t = min(n, pref)
    while n % t:
        t -= 64
    return t


def _sigmoid(z):
    return 1.0 / (1.0 + jnp.exp(-z))


def _silu(z):
    return z * _sigmoid(z)


def _gelu_tanh(z):
    return 0.5 * z * (1.0 + jnp.tanh(math.sqrt(2.0 / math.pi) * (z + 0.044715 * (z * z * z))))


def _layer_norm(z, g, b):
    mu = jnp.mean(z, axis=-1, keepdims=True)
    d = z - mu
    var = jnp.mean(d * d, axis=-1, keepdims=True)
    return d * jax.lax.rsqrt(var + LN_EPS) * g + b


def _dot(a, b):
    return jnp.dot(a, b, preferred_element_type=F32)


def _dot_nt(a, b):
    return jax.lax.dot_general(a, b, (((1,), (1,)), ((), ())), preferred_element_type=F32)


def _split2(z):
    hi = z.astype(BF16)
    lo = (z - hi.astype(F32)).astype(BF16)
    return hi, lo


def _group_sum(z, bd):
    hi, lo = _split2(z)
    return _dot(hi, bd) + _dot(lo, bd)


def _head_sum(z, bd):
    w = z.shape[-1]
    return jnp.concatenate([_group_sum(z[:, s:s + GROUP_LANES], bd) for s in range(0, w, GROUP_LANES)], axis=1)


def _ada_kernel(c_ref, w_ref, b_ref, o_ref):
    a = _silu(c_ref[...]).astype(BF16)
    o_ref[...] = _dot(a, w_ref[...].astype(BF16)) + b_ref[...]


def _ada(cc, w_ada, b_ada):
    m, d = cc.shape
    n = w_ada.shape[1]
    tn = 1024
    return pl.pallas_call(
        _ada_kernel,
        grid=(n // tn,),
        in_specs=[pl.BlockSpec((m, d), lambda j: (0, 0)),
                  pl.BlockSpec((d, tn), lambda j: (0, j)),
                  pl.BlockSpec((1, tn), lambda j: (0, j))],
        out_specs=pl.BlockSpec((m, tn), lambda j: (0, j)),
        out_shape=jax.ShapeDtypeStruct((m, n), F32),
        compiler_params=_cparams(("arbitrary",)),
        name="ada",
    )(cc, w_ada, b_ada.reshape(1, n))


def _ffn_kernel(x_ref, mod_ref, wig_ref, wiu_ref, wo_ref, g_ref, b_ref, o_ref, h_sc, acc_sc, *, slot):
    f = pl.program_id(1)

    @pl.when(f == 0)
    def _():
        shift = mod_ref[0, 3 * slot:3 * slot + 1, :]
        scale = mod_ref[0, 3 * slot + 1:3 * slot + 2, :]
        h_sc[...] = (x_ref[...] * (1.0 + scale) + shift).astype(BF16)
        acc_sc[...] = jnp.zeros_like(acc_sc)

    h = h_sc[...]
    gate = _dot(h, wig_ref[...])
    up = _dot(h, wiu_ref[...])
    act = (_silu(gate) * up).astype(BF16)
    acc_sc[...] += _dot(act, wo_ref[...])

    @pl.when(f == pl.num_programs(1) - 1)
    def _():
        gmod = mod_ref[0, 3 * slot + 2:3 * slot + 3, :]
        z = ALPHA * x_ref[...] + 0.5 * gmod * acc_sc[...]
        o_ref[...] = _layer_norm(z, g_ref[...], b_ref[...])


def _ffn(x2, mods, mod_row, wi, wo, ln_g, ln_b, slot, tm):
    ntok, d = x2.shape
    dff = wo.shape[0]
    tf = 512
    nf = dff // tf
    return pl.pallas_call(
        functools.partial(_ffn_kernel, slot=slot),
        grid=(ntok // tm, nf),
        in_specs=[pl.BlockSpec((tm, d), lambda i, f: (i, 0)),
                  pl.BlockSpec((1, N_MOD, d), lambda i, f: (mod_row(i), 0, 0)),
                  pl.BlockSpec((d, tf), lambda i, f: (0, f)),
                  pl.BlockSpec((d, tf), lambda i, f: (0, nf + f)),
                  pl.BlockSpec((tf, d), lambda i, f: (f, 0)),
                  pl.BlockSpec((1, d), lambda i, f: (0, 0)),
                  pl.BlockSpec((1, d), lambda i, f: (0, 0))],
        out_specs=pl.BlockSpec((tm, d), lambda i, f: (i, 0)),
        out_shape=jax.ShapeDtypeStruct((ntok, d), F32),
        scratch_shapes=[pltpu.VMEM((tm, d), BF16), pltpu.VMEM((tm, d), F32)],
        compiler_params=_cparams(("parallel", "arbitrary")),
        name=f"ffn{slot}",
    )(x2, mods, wi, wi, wo, ln_g.reshape(1, d), ln_b.reshape(1, d))


def _inproj_kernel(x_ref, mod_ref, w_ref, o_ref, *, slot):
    shift = mod_ref[0, 3 * slot:3 * slot + 1, :]
    scale = mod_ref[0, 3 * slot + 1:3 * slot + 2, :]
    h = (x_ref[...] * (1.0 + scale) + shift).astype(BF16)
    o_ref[...] = _dot(h, w_ref[...])


def _inproj(x2, mods, mod_row, w, slot, tm, tn):
    ntok, d = x2.shape
    n = w.shape[1]
    return pl.pallas_call(
        functools.partial(_inproj_kernel, slot=slot),
        grid=(n // tn, ntok // tm),
        in_specs=[pl.BlockSpec((tm, d), lambda j, i: (i, 0)),
                  pl.BlockSpec((1, N_MOD, d), lambda j, i: (mod_row(i), 0, 0)),
                  pl.BlockSpec((d, tn), lambda j, i: (0, j))],
        out_specs=pl.BlockSpec((tm, tn), lambda j, i: (i, j)),
        out_shape=jax.ShapeDtypeStruct((ntok, n), F32),
        compiler_params=_cparams(("parallel", "parallel")),
        name="inproj",
    )(x2, mods, w)


def _prep_kernel(*refs, grid_mode, tiles_per_seq, rw):
    if grid_mode:
        p_ref, up_ref, dn_ref = refs[:3]
        refs = refs[3:]
    else:
        p_ref = refs[0]
        refs = refs[1:]
    (mu_ref, w0_ref, wup_ref, a0_ref, aup_ref, gup_ref, kk_ref, ka_ref, bd_ref,
     r_o, v_o, g_o, kk_o, lw0_o, lw1_o, kd0_o, kd1_o, ic0_o, ic1_o) = refs
    p = p_ref[...]
    tm, width = p.shape
    row = jax.lax.broadcasted_iota(jnp.int32, (tm, width), 0)
    lane = jax.lax.broadcasted_iota(jnp.int32, (tm, width), 1)
    prev = pltpu.roll(p, 1, axis=0)
    nxt = pltpu.roll(p, tm - 1, axis=0)
    if grid_mode:
        i = pl.program_id(0) % tiles_per_seq
        col = row % GRID_W
        left = jnp.where(col > 0, prev, 0.0)
        right = jnp.where(col < GRID_W - 1, nxt, 0.0)
        up_halo = jnp.where(i > 0, up_ref[...], 0.0)
        dn_halo = jnp.where(i < tiles_per_seq - 1, dn_ref[...], 0.0)
        up = jnp.concatenate([up_halo, p[:tm - GRID_W]], axis=0)
        down = jnp.concatenate([p[GRID_W:], dn_halo], axis=0)
        c4 = lane % 4
        shifted = jnp.where(c4 == 0, left, jnp.where(c4 == 1, right, jnp.where(c4 == 2, up, down)))
    else:
        prev = jnp.where(row > 0, prev, 0.0)
        nxt = jnp.where(row < tm - 1, nxt, 0.0)
        shifted = jnp.where(lane % 2 == 0, prev, nxt)
    xs = p + (shifted - p) * mu_ref[...]
    r = xs[:, 0:rw]
    k = xs[:, rw:2 * rw]
    v = xs[:, 2 * rw:3 * rw]
    wa_lo = xs[:, 3 * rw:3 * rw + 128]
    g_lo = xs[:, 3 * rw + 128:3 * rw + 256]
    tanh_wa = jnp.tanh(wa_lo).astype(BF16)
    wa_bf = wa_lo.astype(BF16)
    r_o[...] = r
    v_o[...] = v
    g_o[...] = _dot(_sigmoid(g_lo).astype(BF16), gup_ref[...])
    kk = k * kk_ref[...]
    ss = _head_sum(kk * kk, bd_ref[...])
    kk_o[...] = kk / jnp.maximum(jnp.sqrt(ss), 1e-12)
    for d, (lw_o, kd_o, ic_o) in enumerate(((lw0_o, kd0_o, ic0_o), (lw1_o, kd1_o, ic1_o))):
        z = w0_ref[d:d + 1, :] + _dot(tanh_wa, wup_ref[d])
        lw_o[...] = -math.exp(-0.5) * _sigmoid(z)
        iclr = _sigmoid(a0_ref[d:d + 1, :] + _dot(wa_bf, aup_ref[d]))
        ic_o[...] = iclr
        kd_o[...] = k * (1.0 + (iclr - 1.0) * ka_ref[...])


def _prep(p_rw, params, bd, grid_mode, tokens_per_seq, tm):
    ntok, width = p_rw.shape
    rw = params["w0"].shape[1]
    tiles_per_seq = tokens_per_seq // tm
    hb = tm // GRID_W
    nhalo = ntok // GRID_W
    const = lambda shape: pl.BlockSpec(shape, lambda i: (0,) * len(shape))
    in_specs = [pl.BlockSpec((tm, width), lambda i: (i, 0))]
    args = [p_rw]
    if grid_mode:
        in_specs += [pl.BlockSpec((GRID_W, width), lambda i: (jnp.maximum(i * hb - 1, 0), 0)),
                     pl.BlockSpec((GRID_W, width), lambda i: (jnp.minimum((i + 1) * hb, nhalo - 1), 0))]
        args += [p_rw, p_rw]
    plist = [params[k] for k in ("mu", "w0", "w_up", "a0", "a_up", "g_up", "k_k", "k_a")] + [bd]
    in_specs += [const(a.shape) for a in plist]
    out_spec = pl.BlockSpec((tm, rw), lambda i: (i, 0))
    return pl.pallas_call(
        functools.partial(_prep_kernel, grid_mode=grid_mode, tiles_per_seq=tiles_per_seq, rw=rw),
        grid=(ntok // tm,),
        in_specs=in_specs,
        out_specs=[out_spec] * 10,
        out_shape=[jax.ShapeDtypeStruct((ntok, rw), F32)] * 10,
        compiler_params=_cparams(("parallel",)),
        name="prep_grid" if grid_mode else "prep_seq",
    )(*args, *plist)


def _stackmask(z, lane_head):
    return jnp.concatenate([jnp.where(lane_head == h, z, jnp.zeros_like(z)) for h in range(HEADS_PER_GROUP)],
                           axis=0)


def _scan_chunk(units, with_y):
    c, l = units[0]["r"].shape
    gc = HEADS_PER_GROUP * c
    n_lvl = int(math.log2(c))
    t = jax.lax.broadcasted_iota(jnp.int32, (c, gc), 0)
    tj = jax.lax.broadcasted_iota(jnp.int32, (c, gc), 1) % c
    lane_head = jax.lax.broadcasted_iota(jnp.int32, (c, l), 1) // RWKV_HEAD
    bd = functools.partial(_stackmask, lane_head=lane_head)
    eye = jnp.where(tj == t, 1.0, 0.0)

    for w in units:
        w["at"] = (-w["kk"] * w["dexc"]).astype(BF16)
        w["rt"] = (w["r"] * w["dinc"]).astype(BF16)
        bt = (w["b"] * w["dinv"]).astype(BF16)
        kt = (w["kd"] * w["dinv"]).astype(BF16)
        lhs = jnp.concatenate([w["at"], w["rt"]], axis=0)
        rhs = jnp.concatenate([bd(bt), bd(kt)], axis=0)
        w["a_all"] = _dot_nt(lhs, rhs)
    for w in units:
        w["s_bf"] = w["s"].astype(BF16)
        w["vbd"] = bd(w["v"].astype(BF16))
        w["u_s"] = _dot_nt(w["at"], w["s_bf"])
        if with_y:
            w["y_s"] = _dot_nt(w["rt"], w["s_bf"])
    for w in units:
        strict = (tj > t) if w["reverse"] else (tj < t)
        a_ab = jnp.where(strict, w["a_all"][:c, :gc], 0.0)
        a_ak = jnp.where(strict, w["a_all"][:c, gc:], 0.0).astype(BF16)
        w["u"] = w["u_s"] + _dot(a_ak, w["vbd"])
        w["p"] = eye + a_ab
        w["apow"] = a_ab.astype(BF16)
    for w in units:
        w["apow"] = _dot(w["apow"], bd(w["apow"])).astype(BF16)
    for lvl in range(1, n_lvl):
        last = lvl == n_lvl - 1
        for w in units:
            p_bf = w["p"].astype(BF16)
            if last:
                w["p"] = w["p"] + _dot(p_bf, bd(w["apow"]))
            else:
                both = _dot(jnp.concatenate([p_bf, w["apow"]], axis=0), bd(w["apow"]))
                w["p"] = w["p"] + both[:c]
                w["apow"] = both[c:].astype(BF16)
    for w in units:
        w["sa"] = _dot(w["p"].astype(BF16), bd(w["u"].astype(BF16)))
    outs = []
    rh = jax.lax.broadcasted_iota(jnp.int32, (l, l), 0) // RWKV_HEAD
    ch = jax.lax.broadcasted_iota(jnp.int32, (l, l), 1) // RWKV_HEAD
    for w in units:
        y = None
        if with_y:
            incl = (tj >= t) if w["reverse"] else (tj <= t)
            a_r = jnp.concatenate([jnp.where(incl, w["a_all"][c:, :gc], 0.0),
                                   jnp.where(incl, w["a_all"][c:, gc:], 0.0)], axis=1).astype(BF16)
            y = w["y_s"] + _dot(a_r, jnp.concatenate([bd(w["sa"].astype(BF16)), w["vbd"]], axis=0))
        sav_t = jnp.concatenate([w["sa"], w["v"]], axis=0).T.astype(BF16)
        bk = jnp.concatenate([w["b"] * w["dend"], w["kd"] * w["dend"]], axis=0).astype(BF16)
        upd = _dot(sav_t, bk)
        outs.append((w["s"] * w["dtot"] + jnp.where(rh == ch, upd, 0.0), y))
    return outs


def _scan_kernel(*refs, with_y):
    ins = refs[:13]
    s0_ref = ins[12]
    if with_y:
        yf_ref, yb_ref, sfin_ref, s_sc = refs[13:]
        y_refs = (yf_ref, yb_ref)
    else:
        sfin_ref, s_sc = refs[13:]
        y_refs = (None, None)
    ci = pl.program_id(1)

    @pl.when(ci == 0)
    def _():
        s_sc[...] = s0_ref[0]

    units = []
    for d in range(2):
        r_ref, v_ref, kk_ref, lw_ref, kd_ref, ic_ref = ins[6 * d:6 * d + 6]
        reverse = d == 1
        lw = lw_ref[0]
        c, width = lw.shape
        ti = jax.lax.broadcasted_iota(jnp.int32, (c, c), 0)
        tj = jax.lax.broadcasted_iota(jnp.int32, (c, c), 1)
        tri = jnp.where((tj >= ti) if reverse else (tj <= ti), 1.0, 0.0).astype(BF16)
        hi = lw.astype(BF16)
        rem = lw - hi.astype(F32)
        mid = rem.astype(BF16)
        lo = (rem - mid.astype(F32)).astype(BF16)
        cum3 = _dot(tri, jnp.concatenate([hi, mid, lo], axis=1))
        cum = cum3[:, :width] + cum3[:, width:2 * width] + cum3[:, 2 * width:]
        tot = cum[0:1, :] if reverse else cum[c - 1:c, :]
        full = dict(r=r_ref[0], v=v_ref[0], kk=kk_ref[0], kd=kd_ref[0], b=kk_ref[0] * ic_ref[0],
                    dinc=jnp.exp(cum), dexc=jnp.exp(cum - lw), dinv=jnp.exp(-cum), dend=jnp.exp(tot - cum),
                    dtot=jnp.exp(tot))
        for g in range(width // GROUP_LANES):
            sl = slice(g * GROUP_LANES, (g + 1) * GROUP_LANES)
            w = {k: a[:, sl] for k, a in full.items()}
            w.update(s=s_sc[d, g], reverse=reverse, d=d, g=g, sl=sl)
            units.append(w)

    for w, (s_new, y) in zip(units, _scan_chunk(units, with_y)):
        s_sc[w["d"], w["g"]] = s_new
        if with_y:
            y_refs[w["d"]][0, :, w["sl"]] = y

    @pl.when(ci == pl.num_programs(1) - 1)
    def _():
        sfin_ref[0] = s_sc[...]


def _scan(q, s0, with_y):
    bsz, t, rw = q["r"].shape
    c = SCAN_CHUNK
    nc = t // c
    ng = rw // GROUP_LANES
    fwd = pl.BlockSpec((1, c, rw), lambda b, i: (b, i, 0))
    bwd = pl.BlockSpec((1, c, rw), lambda b, i: (b, nc - 1 - i, 0))
    s_spec = pl.BlockSpec((1, 2, ng, GROUP_LANES, GROUP_LANES), lambda b, i: (b, 0, 0, 0, 0))
    args = [q["r"], q["v"], q["kk"], q["lw0"], q["kd0"], q["ic0"],
            q["r"], q["v"], q["kk"], q["lw1"], q["kd1"], q["ic1"], s0]
    in_specs = [fwd] * 6 + [bwd] * 6 + [s_spec]
    y_shape = jax.ShapeDtypeStruct((bsz, t, rw), F32)
    out_specs = ([fwd, bwd] if with_y else []) + [s_spec]
    out_shape = ([y_shape, y_shape] if with_y else []) + [jax.ShapeDtypeStruct(s0.shape, F32)]
    outs = pl.pallas_call(
        functools.partial(_scan_kernel, with_y=with_y),
        grid=(bsz, nc),
        in_specs=in_specs,
        out_specs=out_specs,
        out_shape=out_shape,
        scratch_shapes=[pltpu.VMEM((2, ng, GROUP_LANES, GROUP_LANES), F32)],
        compiler_params=_cparams(("parallel", "arbitrary")),
        name="scan_y" if with_y else "scan_state",
    )(*args)
    return outs


def _mixout_kernel(yf_ref, yb_ref, r_ref, v_ref, g_ref, kd0_ref, kd1_ref, gm_ref, x_ref, mod_ref,
                   rk_ref, gng_ref, gnb_ref, glg_ref, glb_ref, ws_ref, bs_ref, wout_ref, lng_ref, lnb_ref,
                   bd_ref, o_ref, *, slot):
    bd = bd_ref[...]
    rw = yf_ref.shape[-1]
    inv_n = 1.0 / RWKV_HEAD
    y = yf_ref[...] + yb_ref[...]
    mu = _head_sum(y, bd) * inv_n
    dy = y - mu
    var = _head_sum(dy * dy, bd) * inv_n
    yn = dy * jax.lax.rsqrt(var + GN_EPS) * gng_ref[...] + gnb_ref[...]
    r = r_ref[...]
    bonus = _head_sum(r * (kd0_ref[...] + kd1_ref[...]) * rk_ref[...], bd) * v_ref[...]
    out_r = ((yn + bonus) * g_ref[...]).astype(BF16)

    gm = _gelu_tanh(gm_ref[...])
    u = gm[:, :rw]
    vv = gm[:, rw:]
    inv_g = 1.0 / GMLP_GROUP
    mu_v = _head_sum(vv, bd) * inv_g
    dv = vv - mu_v
    var_v = _head_sum(dv * dv, bd) * inv_g
    vn = (dv * jax.lax.rsqrt(var_v + LN_EPS) * glg_ref[...] + glb_ref[...]).astype(BF16)
    tm = vn.shape[0]
    lane = jax.lax.broadcasted_iota(jnp.int32, (CHUNK, 2 * GMLP_GROUP), 1)
    chunks = []
    for n in range(tm // CHUNK):
        pairs = []
        for gp in range(rw // (2 * GMLP_GROUP)):
            v2 = vn[n * CHUNK:(n + 1) * CHUNK, gp * 2 * GMLP_GROUP:(gp + 1) * 2 * GMLP_GROUP]
            m0 = _dot(ws_ref[2 * gp], v2)
            m1 = _dot(ws_ref[2 * gp + 1], v2)
            pairs.append(jnp.where(lane < GMLP_GROUP, m0, m1))
        chunks.append(jnp.concatenate(pairs, axis=1) + bs_ref[...])
    mixed = jnp.concatenate(chunks, axis=0)
    out_g = (u * mixed).astype(BF16)

    o = _dot(out_r, wout_ref[0:rw, :]) + _dot(out_g, wout_ref[rw:, :])
    gate = mod_ref[0, 3 * slot + 2:3 * slot + 3, :]
    z = ALPHA * x_ref[...] + gate * o
    o_ref[...] = _layer_norm(z, lng_ref[...], lnb_ref[...])


def _mixout(yf, yb, q, p_gm, x2, mods, mod_row, params, bd, slot, tm):
    ntok, d = x2.shape
    rw = yf.shape[-1]
    tok = lambda w: pl.BlockSpec((tm, w), lambda i: (i, 0))
    const = lambda a: pl.BlockSpec(a.shape, lambda i: (0,) * a.ndim)
    plist = [params[k] for k in ("r_k", "gn_g", "gn_b", "gm_ln_g", "gm_ln_b", "gm_ws", "gm_bs_full", "w_out",
                                 "ln_g", "ln_b")] + [bd]
    return pl.pallas_call(
        functools.partial(_mixout_kernel, slot=slot),
        grid=(ntok // tm,),
        in_specs=[tok(rw)] * 7 + [tok(p_gm.shape[1]), tok(d),
                                  pl.BlockSpec((1, N_MOD, d), lambda i: (mod_row(i), 0, 0))]
                 + [const(a) for a in plist],
        out_specs=tok(d),
        out_shape=jax.ShapeDtypeStruct((ntok, d), F32),
        compiler_params=_cparams(("parallel",)),
        name="mixout",
    )(yf, yb, q["r"], q["v"], q["g"], q["kd0"], q["kd1"], p_gm, x2, mods, *plist)


_PREP_NAMES = ("r", "v", "g", "kk", "lw0", "lw1", "kd0", "kd1", "ic0", "ic1")


def kernel(x, c, ctx, c_ctx, w_ada, b_ada, ln_g, ln_b, ffn_a_wi, ffn_a_wo, ffn_b_wi, ffn_b_wo, w_in, mu_shift,
           w0, w_up, a0, a_up, g_up, k_k, k_a, r_k, gn_g, gn_b, gm_ln_g, gm_ln_b, gm_ws, gm_bs, w_out):
    bsz, seq, d = x.shape
    ctx_len = ctx.shape[1]
    assert w_ada.shape[0] == DEPTH
    rw = w0.shape[-1]
    rwkv_in = mu_shift.shape[-1]
    w_lora, a_lora = w_up.shape[2], a_up.shape[2]
    assert w_lora + a_lora == 128 and g_up.shape[1] == 128 and rwkv_in == 3 * rw + 256
    assert seq % GRID_W == 0 and seq % SCAN_CHUNK == 0 and ctx_len % SCAN_CHUNK == 0 and seq % CHUNK == 0
    i = 0

    n_rows = -(-(bsz + 1) // 8) * 8
    cc = jnp.zeros((n_rows, d), F32).at[:bsz].set(c).at[bsz].set(c_ctx)
    mods = _ada(cc, w_ada[i], b_ada[i]).reshape(n_rows, N_MOD, d)

    x2 = x.reshape(bsz * seq, d)
    c2 = ctx.reshape(bsz * ctx_len, d)
    tm_x = _pick(seq, 512)
    tm_c = _pick(ctx_len, 512)
    row_x = lambda tm: (lambda t: t // (seq // tm))
    row_c = lambda t: bsz

    bf = lambda a: a.astype(BF16)
    wi_a, wo_a = bf(ffn_a_wi[i]), bf(ffn_a_wo[i])
    x1 = _ffn(x2, mods, row_x(tm_x), wi_a, wo_a, ln_g[i, 0], ln_b[i, 0], 0, tm_x)
    c1 = _ffn(c2, mods, row_c, wi_a, wo_a, ln_g[i, 0], ln_b[i, 0], 0, tm_c)

    w_in_bf = bf(w_in[i])
    w_rw, w_gm = w_in_bf[:, :rwkv_in], w_in_bf[:, rwkv_in:]
    p_rw_x = _inproj(x1, mods, row_x(tm_x), w_rw, 1, tm_x, rwkv_in // 2)
    p_gm_x = _inproj(x1, mods, row_x(tm_x), w_gm, 1, tm_x, w_gm.shape[1] // 2)
    p_rw_c = _inproj(c1, mods, row_c, w_rw, 1, tm_c, rwkv_in // 2)

    zpad = lambda a, lo, hi: jnp.pad(a, ((0, 0), (lo, hi), (0, 0)))
    lanes = jnp.arange(GROUP_LANES) // RWKV_HEAD
    bd = (lanes[:, None] == lanes[None, :]).astype(BF16)
    prm = dict(mu=mu_shift[i].reshape(1, rwkv_in), w0=w0[i], w_up=bf(zpad(w_up[i], 0, a_lora)), a0=a0[i],
               a_up=bf(zpad(a_up[i], w_lora, 0)), g_up=bf(g_up[i]), k_k=k_k[i].reshape(1, rw),
               k_a=k_a[i].reshape(1, rw))
    tm_p = _pick(seq, 256)
    q_x = dict(zip(_PREP_NAMES, _prep(p_rw_x, prm, bd, True, seq, tm_p)))
    q_c = dict(zip(_PREP_NAMES, _prep(p_rw_c, prm, bd, False, ctx_len, ctx_len)))
    q_x3 = {k: a.reshape(bsz, seq, rw) for k, a in q_x.items()}
    q_c3 = {k: a.reshape(bsz, ctx_len, rw) for k, a in q_c.items()}

    ng = rw // GROUP_LANES
    s_zero = jnp.zeros((bsz, 2, ng, GROUP_LANES, GROUP_LANES), F32)
    (s_ctx,) = _scan(q_c3, s_zero, with_y=False)
    yf, yb, _ = _scan(q_x3, s_ctx, with_y=True)

    out_prm = dict(r_k=r_k[i].reshape(1, rw), gn_g=gn_g[i].reshape(1, rw), gn_b=gn_b[i].reshape(1, rw),
                   gm_ln_g=gm_ln_g[i].reshape(1, rw), gm_ln_b=gm_ln_b[i].reshape(1, rw), gm_ws=bf(gm_ws[i]),
                   gm_bs_full=jnp.repeat(gm_bs[i].T, GMLP_GROUP, axis=1), w_out=bf(w_out[i]),
                   ln_g=ln_g[i, 1].reshape(1, d), ln_b=ln_b[i, 1].reshape(1, d))
    tm_o = _pick(seq, 256)
    x2b = _mixout(yf.reshape(bsz * seq, rw), yb.reshape(bsz * seq, rw), q_x, p_gm_x, x1, mods, row_x(tm_o),
                  out_prm, bd, 1, tm_o)

    x3 = _ffn(x2b, mods, row_x(tm_x), bf(ffn_b_wi[i]), bf(ffn_b_wo[i]), ln_g[i, 2], ln_b[i, 2], 2, tm_x)
    return x3.reshape(bsz, seq, d)
```

```python
import functools
import math

import jax
import jax.numpy as jnp
from jax.experimental import pallas as pl
from jax.experimental.pallas import tpu as pltpu

F32 = jnp.float32
BF16 = jnp.bfloat16

GRID_W = 64
RWKV_HEAD = 64
CHUNK = 128
GMLP_GROUP = 64
N_MOD = 9
LN_EPS = 1e-5
GN_EPS = 64e-5
DEPTH = 1
ALPHA = (2.0 * DEPTH) ** 0.25

SCAN_CHUNK = 64
HEADS_PER_GROUP = 4
GROUP_LANES = HEADS_PER_GROUP * RWKV_HEAD
VMEM_LIMIT = 56 * 1024 * 1024


def _cparams(sem):
    return pltpu.CompilerParams(dimension_semantics=sem, vmem_limit_bytes=VMEM_LIMIT)


def _pick(n, pref):
    t = min(n, pref)
    while n % t:
        t -= 64
    return t


def _sigmoid(z):
    return 1.0 / (1.0 + jnp.exp(-z))


def _silu(z):
    return z * _sigmoid(z)


def _gelu_tanh(z):
    return 0.5 * z * (1.0 + jnp.tanh(math.sqrt(2.0 / math.pi) * (z + 0.044715 * (z * z * z))))


def _layer_norm(z, g, b):
    mu = jnp.mean(z, axis=-1, keepdims=True)
    d = z - mu
    var = jnp.mean(d * d, axis=-1, keepdims=True)
    return d * jax.lax.rsqrt(var + LN_EPS) * g + b


def _dot(a, b):
    return jnp.dot(a, b, preferred_element_type=F32)


def _dot_nt(a, b):
    return jax.lax.dot_general(a, b, (((1,), (1,)), ((), ())), preferred_element_type=F32)


def _group_sum(z, bd):
    return _dot(z.astype(BF16), bd)


def _head_sum(z, bd):
    w = z.shape[-1]
    return jnp.concatenate([_group_sum(z[:, s:s + GROUP_LANES], bd) for s in range(0, w, GROUP_LANES)], axis=1)


def _ada_kernel(c_ref, w_ref, b_ref, o_ref):
    a = _silu(c_ref[...]).astype(BF16)
    o_ref[...] = _dot(a, w_ref[...].astype(BF16)) + b_ref[...]


def _ada(cc, w_ada, b_ada):
    m, d = cc.shape
    n = w_ada.shape[1]
    tn = 1024
    return pl.pallas_call(
        _ada_kernel,
        grid=(n // tn,),
        in_specs=[pl.BlockSpec((m, d), lambda j: (0, 0)),
                  pl.BlockSpec((d, tn), lambda j: (0, j)),
                  pl.BlockSpec((1, tn), lambda j: (0, j))],
        out_specs=pl.BlockSpec((m, tn), lambda j: (0, j)),
        out_shape=jax.ShapeDtypeStruct((m, n), F32),
        compiler_params=_cparams(("arbitrary",)),
        name="ada",
    )(cc, w_ada, b_ada.reshape(1, n))


def _ffn_kernel(x_ref, mod_ref, wi_ref, wo_ref, g_ref, b_ref, o_ref, h_sc, acc_sc, *, slot):
    f = pl.program_id(1)

    @pl.when(f == 0)
    def _():
        shift = mod_ref[0, 3 * slot:3 * slot + 1, :]
        scale = mod_ref[0, 3 * slot + 1:3 * slot + 2, :]
        h_sc[...] = (x_ref[...] * (1.0 + scale) + shift).astype(BF16)
        acc_sc[...] = jnp.zeros_like(acc_sc)

    tf = wo_ref.shape[0]
    gate_up = _dot(h_sc[...], wi_ref[...])
    act = (_silu(gate_up[:, :tf]) * gate_up[:, tf:]).astype(BF16)
    acc_sc[...] += _dot(act, wo_ref[...])

    @pl.when(f == pl.num_programs(1) - 1)
    def _():
        gmod = mod_ref[0, 3 * slot + 2:3 * slot + 3, :]
        z = ALPHA * x_ref[...] + 0.5 * gmod * acc_sc[...]
        o_ref[...] = _layer_norm(z, g_ref[...], b_ref[...])


FFN_TF = 512


def _ffn_weights(wi, wo):
    d, dff2 = wi.shape
    nf = dff2 // (2 * FFN_TF)
    wi_r = wi.reshape(d, 2, nf, FFN_TF).transpose(0, 2, 1, 3).reshape(d, dff2)
    return wi_r.astype(BF16), wo.astype(BF16)


def _ffn(x2, mods, mod_row, wi, wo, ln_g, ln_b, slot, tm):
    ntok, d = x2.shape
    dff = wo.shape[0]
    tf = FFN_TF
    nf = dff // tf
    return pl.pallas_call(
        functools.partial(_ffn_kernel, slot=slot),
        grid=(ntok // tm, nf),
        in_specs=[pl.BlockSpec((tm, d), lambda i, f: (i, 0)),
                  pl.BlockSpec((1, N_MOD, d), lambda i, f: (mod_row(i), 0, 0)),
                  pl.BlockSpec((d, 2 * tf), lambda i, f: (0, f)),
                  pl.BlockSpec((tf, d), lambda i, f: (f, 0)),
                  pl.BlockSpec((1, d), lambda i, f: (0, 0)),
                  pl.BlockSpec((1, d), lambda i, f: (0, 0))],
        out_specs=pl.BlockSpec((tm, d), lambda i, f: (i, 0)),
        out_shape=jax.ShapeDtypeStruct((ntok, d), F32),
        scratch_shapes=[pltpu.VMEM((tm, d), BF16), pltpu.VMEM((tm, d), F32)],
        compiler_params=_cparams(("parallel", "arbitrary")),
        name=f"ffn{slot}",
    )(x2, mods, wi, wo, ln_g.reshape(1, d), ln_b.reshape(1, d))


def _inproj_kernel(x_ref, mod_ref, w_ref, o_ref, *, slot):
    shift = mod_ref[0, 3 * slot:3 * slot + 1, :]
    scale = mod_ref[0, 3 * slot + 1:3 * slot + 2, :]
    h = (x_ref[...] * (1.0 + scale) + shift).astype(BF16)
    o_ref[...] = _dot(h, w_ref[...])


def _inproj(x2, mods, mod_row, w, slot, tm, tn):
    ntok, d = x2.shape
    n = w.shape[1]
    return pl.pallas_call(
        functools.partial(_inproj_kernel, slot=slot),
        grid=(n // tn, ntok // tm),
        in_specs=[pl.BlockSpec((tm, d), lambda j, i: (i, 0)),
                  pl.BlockSpec((1, N_MOD, d), lambda j, i: (mod_row(i), 0, 0)),
                  pl.BlockSpec((d, tn), lambda j, i: (0, j))],
        out_specs=pl.BlockSpec((tm, tn), lambda j, i: (i, j)),
        out_shape=jax.ShapeDtypeStruct((ntok, n), F32),
        compiler_params=_cparams(("parallel", "parallel")),
        name="inproj",
    )(x2, mods, w)


_PREP_NAMES = ("r", "v", "g", "kk", "kd0", "kd1", "ic0", "ic1", "lw0", "lw1")
_PREP_DTYPES = (BF16,) * 8 + (F32,) * 2


def _prep_kernel(*refs, grid_mode, tiles_per_seq, rw, slot):
    if grid_mode:
        p_ref, up_ref, dn_ref, x_ref, mod_ref, wgm_ref = refs[:6]
        refs = refs[6:]
    else:
        p_ref = refs[0]
        refs = refs[1:]
    (mu_ref, w0_ref, wup_ref, a0_ref, aup_ref, gup_ref, kk_ref, ka_ref, bd_ref,
     r_o, v_o, g_o, kk_o, kd0_o, kd1_o, ic0_o, ic1_o, lw0_o, lw1_o) = refs[:19]
    tm = p_ref.shape[0]
    n_gm = 4

    def gm_chunk(j):
        if grid_mode:
            pgm_o = refs[19]
            wn = pgm_o.shape[1] // n_gm
            pgm_o[:, j * wn:(j + 1) * wn] = _dot(h_gm, wgm_ref[:, j * wn:(j + 1) * wn]).astype(BF16)

    def mixed(lo, hi):
        p = p_ref[:, lo:hi]
        width = hi - lo
        row = jax.lax.broadcasted_iota(jnp.int32, (tm, width), 0)
        lane = jax.lax.broadcasted_iota(jnp.int32, (tm, width), 1)
        prev = pltpu.roll(p, 1, axis=0)
        nxt = pltpu.roll(p, tm - 1, axis=0)
        if grid_mode:
            i = pl.program_id(0) % tiles_per_seq
            col = row % GRID_W
            left = jnp.where(col > 0, prev, 0.0)
            right = jnp.where(col < GRID_W - 1, nxt, 0.0)
            up_halo = jnp.where(i > 0, up_ref[:, lo:hi], 0.0)
            dn_halo = jnp.where(i < tiles_per_seq - 1, dn_ref[:, lo:hi], 0.0)
            up = jnp.concatenate([up_halo, p[:tm - GRID_W]], axis=0)
            down = jnp.concatenate([p[GRID_W:], dn_halo], axis=0)
            c4 = lane % 4
            shifted = jnp.where(c4 == 0, left, jnp.where(c4 == 1, right, jnp.where(c4 == 2, up, down)))
        else:
            prev = jnp.where(row > 0, prev, 0.0)
            nxt = jnp.where(row < tm - 1, nxt, 0.0)
            shifted = jnp.where(lane % 2 == 0, prev, nxt)
        return p + (shifted - p) * mu_ref[:, lo:hi]

    if grid_mode:
        shift = mod_ref[0, 3 * slot:3 * slot + 1, :]
        scale = mod_ref[0, 3 * slot + 1:3 * slot + 2, :]
        h_gm = (x_ref[...] * (1.0 + scale) + shift).astype(BF16)
    wa_lo = mixed(3 * rw, 3 * rw + 128)
    g_lo = mixed(3 * rw + 128, 3 * rw + 256)
    tanh_wa = jnp.tanh(wa_lo).astype(BF16)
    wa_bf = wa_lo.astype(BF16)
    sig_g = _sigmoid(g_lo).astype(BF16)
    gm_chunk(0)
    k = mixed(rw, 2 * rw)
    kk = k * kk_ref[...]
    kk_sq = kk * kk
    g_lin = _dot(sig_g, gup_ref[...])
    z_w = [_dot(tanh_wa, wup_ref[d]) for d in range(2)]
    z_a = [_dot(wa_bf, aup_ref[d]) for d in range(2)]
    gm_chunk(1)
    ss = _head_sum(kk_sq, bd_ref[...])
    gm_chunk(2)
    g_o[...] = g_lin.astype(BF16)
    r_o[...] = mixed(0, rw).astype(BF16)
    v_o[...] = mixed(2 * rw, 3 * rw).astype(BF16)
    kk_o[...] = (kk / jnp.maximum(jnp.sqrt(ss), 1e-12)).astype(BF16)
    for d, (lw_o, kd_o, ic_o) in enumerate(((lw0_o, kd0_o, ic0_o), (lw1_o, kd1_o, ic1_o))):
        lw_o[...] = -math.exp(-0.5) * _sigmoid(w0_ref[d:d + 1, :] + z_w[d])
        iclr = _sigmoid(a0_ref[d:d + 1, :] + z_a[d])
        ic_o[...] = iclr.astype(BF16)
        kd_o[...] = (k * (1.0 + (iclr - 1.0) * ka_ref[...])).astype(BF16)
    gm_chunk(3)


def _prep(p_rw, params, bd, tokens_per_seq, tm, gm=None):
    ntok, width = p_rw.shape
    rw = params["w0"].shape[1]
    grid_mode = gm is not None
    tiles_per_seq = tokens_per_seq // tm
    hb = tm // GRID_W
    nhalo = ntok // GRID_W
    const = lambda a: pl.BlockSpec(a.shape, lambda i: (0,) * a.ndim, pipeline_mode=pl.Buffered(1))
    tok = lambda w: pl.BlockSpec((tm, w), lambda i: (i, 0))
    in_specs = [tok(width)]
    args = [p_rw]
    names, dtypes, widths = list(_PREP_NAMES), list(_PREP_DTYPES), [rw] * 10
    slot = 0
    if grid_mode:
        x2, mods, mod_row, w_gm, slot = gm
        d = x2.shape[1]
        in_specs += [pl.BlockSpec((GRID_W, width), lambda i: (jnp.maximum(i * hb - 1, 0), 0)),
                     pl.BlockSpec((GRID_W, width), lambda i: (jnp.minimum((i + 1) * hb, nhalo - 1), 0)),
                     tok(d), pl.BlockSpec((1, N_MOD, d), lambda i: (mod_row(i), 0, 0)), const(w_gm)]
        args += [p_rw, p_rw, x2, mods, w_gm]
        names.append("p_gm")
        dtypes.append(BF16)
        widths.append(w_gm.shape[1])
    plist = [params[k] for k in ("mu", "w0", "w_up", "a0", "a_up", "g_up", "k_k", "k_a")] + [bd]
    in_specs += [const(a) for a in plist]
    outs = pl.pallas_call(
        functools.partial(_prep_kernel, grid_mode=grid_mode, tiles_per_seq=tiles_per_seq, rw=rw, slot=slot),
        grid=(ntok // tm,),
        in_specs=in_specs,
        out_specs=[tok(w) for w in widths],
        out_shape=[jax.ShapeDtypeStruct((ntok, w), dt) for w, dt in zip(widths, dtypes)],
        compiler_params=_cparams(("parallel",)),
        name="prep_grid" if grid_mode else "prep_seq",
    )(*args, *plist)
    return dict(zip(names, outs))


def _stackmask(z, lane_head):
    return jnp.concatenate([jnp.where(lane_head == h, z, jnp.zeros_like(z)) for h in range(HEADS_PER_GROUP)],
                           axis=0)


def _scan_chunk(units, with_y):
    c, l = units[0]["r"].shape
    gc = HEADS_PER_GROUP * c
    n_lvl = int(math.log2(c))
    t = jax.lax.broadcasted_iota(jnp.int32, (c, gc), 0)
    tj = jax.lax.broadcasted_iota(jnp.int32, (c, gc), 1) % c
    lane_head = jax.lax.broadcasted_iota(jnp.int32, (c, l), 1) // RWKV_HEAD
    bd = functools.partial(_stackmask, lane_head=lane_head)
    eye = jnp.where(tj == t, 1.0, 0.0)

    for w in units:
        w["at"] = (-w["kk"] * w["dexc"]).astype(BF16)
        w["rt"] = (w["r"] * w["dinc"]).astype(BF16)
        bt = (w["b"] * w["dinv"]).astype(BF16)
        kt = (w["kd"] * w["dinv"]).astype(BF16)
        lhs = jnp.concatenate([w["at"], w["rt"]], axis=0) if with_y else w["at"]
        rhs = jnp.concatenate([bd(bt), bd(kt)], axis=0)
        w["a_all"] = _dot_nt(lhs, rhs)
        w["from_s"] = _dot_nt(lhs, w["s"].astype(BF16))
    for w in units:
        strict = (tj > t) if w["reverse"] else (tj < t)
        a_ab = jnp.where(strict, w["a_all"][:c, :gc], 0.0)
        a_v = jnp.where(strict, w["a_all"][:c, gc:], 0.0)
        if with_y:
            incl = (tj >= t) if w["reverse"] else (tj <= t)
            a_v = jnp.concatenate([a_v, jnp.where(incl, w["a_all"][c:, gc:], 0.0)], axis=0)
            w["a_rb"] = jnp.where(incl, w["a_all"][c:, :gc], 0.0).astype(BF16)
        w["from_v"] = _dot(a_v.astype(BF16), bd(w["v"].astype(BF16)))
        w["u"] = w["from_s"][:c] + w["from_v"][:c]
        w["p"] = eye + a_ab
        w["apow"] = a_ab.astype(BF16)
    for w in units:
        w["apow"] = _dot(w["apow"], bd(w["apow"])).astype(BF16)
    for lvl in range(1, n_lvl):
        last = lvl == n_lvl - 1
        for w in units:
            p_bf = w["p"].astype(BF16)
            if last:
                w["p"] = w["p"] + _dot(p_bf, bd(w["apow"]))
            else:
                both = _dot(jnp.concatenate([p_bf, w["apow"]], axis=0), bd(w["apow"]))
                w["p"] = w["p"] + both[:c]
                w["apow"] = both[c:].astype(BF16)
    for w in units:
        w["sa"] = _dot(w["p"].astype(BF16), bd(w["u"].astype(BF16)))
    outs = []
    rh = jax.lax.broadcasted_iota(jnp.int32, (l, l), 0) // RWKV_HEAD
    ch = jax.lax.broadcasted_iota(jnp.int32, (l, l), 1) // RWKV_HEAD
    for w in units:
        y = None
        if with_y:
            y = w["from_s"][c:] + w["from_v"][c:] + _dot(w["a_rb"], bd(w["sa"].astype(BF16)))
        sav_t = jnp.concatenate([w["sa"], w["v"]], axis=0).T.astype(BF16)
        bk = jnp.concatenate([w["b"] * w["dend"], w["kd"] * w["dend"]], axis=0).astype(BF16)
        upd = _dot(sav_t, bk)
        outs.append((w["s"] * w["dtot"] + jnp.where(rh == ch, upd, 0.0), y))
    return outs


def _scan_kernel(*refs, with_y):
    ins = refs[:13]
    s0_ref = ins[12]
    if with_y:
        yf_ref, yb_ref, sfin_ref, s_sc = refs[13:]
        y_refs = (yf_ref, yb_ref)
    else:
        sfin_ref, s_sc = refs[13:]
        y_refs = (None, None)
    ci = pl.program_id(1)

    @pl.when(ci == 0)
    def _():
        s_sc[...] = s0_ref[0]

    units = []
    for d in range(2):
        r_ref, v_ref, kk_ref, lw_ref, kd_ref, ic_ref = ins[6 * d:6 * d + 6]
        reverse = d == 1
        lw = lw_ref[0]
        c, width = lw.shape
        ti = jax.lax.broadcasted_iota(jnp.int32, (c, c), 0)
        tj = jax.lax.broadcasted_iota(jnp.int32, (c, c), 1)
        tri = jnp.where((tj >= ti) if reverse else (tj <= ti), 1.0, 0.0).astype(BF16)
        hi = lw.astype(BF16)
        rem = lw - hi.astype(F32)
        mid = rem.astype(BF16)
        lo = (rem - mid.astype(F32)).astype(BF16)
        cum3 = _dot(tri, jnp.concatenate([hi, mid, lo], axis=1))
        cum = cum3[:, :width] + cum3[:, width:2 * width] + cum3[:, 2 * width:]
        tot = cum[0:1, :] if reverse else cum[c - 1:c, :]
        kk = kk_ref[0].astype(F32)
        full = dict(r=r_ref[0].astype(F32), v=v_ref[0].astype(F32), kk=kk, kd=kd_ref[0].astype(F32),
                    b=kk * ic_ref[0].astype(F32),
                    dinc=jnp.exp(cum), dexc=jnp.exp(cum - lw), dinv=jnp.exp(-cum), dend=jnp.exp(tot - cum),
                    dtot=jnp.exp(tot))
        for g in range(width // GROUP_LANES):
            sl = slice(g * GROUP_LANES, (g + 1) * GROUP_LANES)
            w = {k: a[:, sl] for k, a in full.items()}
            w.update(s=s_sc[d, g], reverse=reverse, d=d, g=g, sl=sl)
            units.append(w)

    for w, (s_new, y) in zip(units, _scan_chunk(units, with_y)):
        s_sc[w["d"], w["g"]] = s_new
        if with_y:
            y_refs[w["d"]][0, :, w["sl"]] = y

    @pl.when(ci == pl.num_programs(1) - 1)
    def _():
        sfin_ref[0] = s_sc[...]


def _scan(q, s0, with_y):
    bsz, t, rw = q["r"].shape
    c = SCAN_CHUNK
    nc = t // c
    ng = rw // GROUP_LANES
    fwd = pl.BlockSpec((1, c, rw), lambda b, i: (b, i, 0))
    bwd = pl.BlockSpec((1, c, rw), lambda b, i: (b, nc - 1 - i, 0))
    s_spec = pl.BlockSpec((1, 2, ng, GROUP_LANES, GROUP_LANES), lambda b, i: (b, 0, 0, 0, 0))
    args = [q["r"], q["v"], q["kk"], q["lw0"], q["kd0"], q["ic0"],
            q["r"], q["v"], q["kk"], q["lw1"], q["kd1"], q["ic1"], s0]
    in_specs = [fwd] * 6 + [bwd] * 6 + [s_spec]
    y_shape = jax.ShapeDtypeStruct((bsz, t, rw), F32)
    out_specs = ([fwd, bwd] if with_y else []) + [s_spec]
    out_shape = ([y_shape, y_shape] if with_y else []) + [jax.ShapeDtypeStruct(s0.shape, F32)]
    outs = pl.pallas_call(
        functools.partial(_scan_kernel, with_y=with_y),
        grid=(bsz, nc),
        in_specs=in_specs,
        out_specs=out_specs,
        out_shape=out_shape,
        scratch_shapes=[pltpu.VMEM((2, ng, GROUP_LANES, GROUP_LANES), F32)],
        compiler_params=_cparams(("parallel", "arbitrary")),
        name="scan_y" if with_y else "scan_state",
    )(*args)
    return outs


def _mixout_kernel(yf_ref, yb_ref, r_ref, v_ref, g_ref, kd0_ref, kd1_ref, gm_ref, x_ref, mod_ref,
                   rk_ref, gng_ref, gnb_ref, glg_ref, glb_ref, ws_ref, bs_ref, wout_ref, lng_ref, lnb_ref,
                   bd_ref, o_ref, *, slot):
    bd = bd_ref[...]
    rw = yf_ref.shape[-1]
    inv_n = 1.0 / RWKV_HEAD
    y = yf_ref[...] + yb_ref[...]
    mu = _head_sum(y, bd) * inv_n
    dy = y - mu
    var = _head_sum(dy * dy, bd) * inv_n
    yn = dy * jax.lax.rsqrt(var + GN_EPS) * gng_ref[...] + gnb_ref[...]
    f32 = lambda ref: ref[...].astype(F32)
    bonus = _head_sum(f32(r_ref) * (f32(kd0_ref) + f32(kd1_ref)) * rk_ref[...], bd) * f32(v_ref)
    out_r = ((yn + bonus) * f32(g_ref)).astype(BF16)

    gm = _gelu_tanh(f32(gm_ref))
    u = gm[:, :rw]
    vv = gm[:, rw:]
    inv_g = 1.0 / GMLP_GROUP
    mu_v = _head_sum(vv, bd) * inv_g
    dv = vv - mu_v
    var_v = _head_sum(dv * dv, bd) * inv_g
    vn = (dv * jax.lax.rsqrt(var_v + LN_EPS) * glg_ref[...] + glb_ref[...]).astype(BF16)
    tm = vn.shape[0]
    lane = jax.lax.broadcasted_iota(jnp.int32, (CHUNK, 2 * GMLP_GROUP), 1)
    chunks = []
    for n in range(tm // CHUNK):
        pairs = []
        for gp in range(rw // (2 * GMLP_GROUP)):
            v2 = vn[n * CHUNK:(n + 1) * CHUNK, gp * 2 * GMLP_GROUP:(gp + 1) * 2 * GMLP_GROUP]
            m0 = _dot(ws_ref[2 * gp], v2)
            m1 = _dot(ws_ref[2 * gp + 1], v2)
            pairs.append(jnp.where(lane < GMLP_GROUP, m0, m1))
        chunks.append(jnp.concatenate(pairs, axis=1) + bs_ref[...])
    mixed = jnp.concatenate(chunks, axis=0)
    out_g = (u * mixed).astype(BF16)

    o = _dot(out_r, wout_ref[0:rw, :]) + _dot(out_g, wout_ref[rw:, :])
    gate = mod_ref[0, 3 * slot + 2:3 * slot + 3, :]
    z = ALPHA * x_ref[...] + gate * o
    o_ref[...] = _layer_norm(z, lng_ref[...], lnb_ref[...])


def _mixout(yf, yb, q, p_gm, x2, mods, mod_row, params, bd, slot, tm):
    ntok, d = x2.shape
    rw = yf.shape[-1]
    tok = lambda w: pl.BlockSpec((tm, w), lambda i: (i, 0))
    const = lambda a: pl.BlockSpec(a.shape, lambda i: (0,) * a.ndim, pipeline_mode=pl.Buffered(1))
    plist = [params[k] for k in ("r_k", "gn_g", "gn_b", "gm_ln_g", "gm_ln_b", "gm_ws", "gm_bs_full", "w_out",
                                 "ln_g", "ln_b")] + [bd]
    return pl.pallas_call(
        functools.partial(_mixout_kernel, slot=slot),
        grid=(ntok // tm,),
        in_specs=[tok(rw)] * 7 + [tok(p_gm.shape[1]), tok(d),
                                  pl.BlockSpec((1, N_MOD, d), lambda i: (mod_row(i), 0, 0))]
                 + [const(a) for a in plist],
        out_specs=tok(d),
        out_shape=jax.ShapeDtypeStruct((ntok, d), F32),
        compiler_params=_cparams(("parallel",)),
        name="mixout",
    )(yf, yb, q["r"], q["v"], q["g"], q["kd0"], q["kd1"], p_gm, x2, mods, *plist)


def kernel(x, c, ctx, c_ctx, w_ada, b_ada, ln_g, ln_b, ffn_a_wi, ffn_a_wo, ffn_b_wi, ffn_b_wo, w_in, mu_shift,
           w0, w_up, a0, a_up, g_up, k_k, k_a, r_k, gn_g, gn_b, gm_ln_g, gm_ln_b, gm_ws, gm_bs, w_out):
    bsz, seq, d = x.shape
    ctx_len = ctx.shape[1]
    assert w_ada.shape[0] == DEPTH
    rw = w0.shape[-1]
    rwkv_in = mu_shift.shape[-1]
    w_lora, a_lora = w_up.shape[2], a_up.shape[2]
    assert w_lora + a_lora == 128 and g_up.shape[1] == 128 and rwkv_in == 3 * rw + 256
    assert seq % GRID_W == 0 and seq % SCAN_CHUNK == 0 and ctx_len % SCAN_CHUNK == 0 and seq % CHUNK == 0
    i = 0

    n_rows = -(-(bsz + 1) // 8) * 8
    cc = jnp.zeros((n_rows, d), F32).at[:bsz].set(c).at[bsz].set(c_ctx)
    mods = _ada(cc, w_ada[i], b_ada[i]).reshape(n_rows, N_MOD, d)

    x2 = x.reshape(bsz * seq, d)
    c2 = ctx.reshape(bsz * ctx_len, d)
    tm_x = _pick(seq, 512)
    tm_c = _pick(ctx_len, 512)
    row_x = lambda tm: (lambda t: t // (seq // tm))
    row_c = lambda t: bsz

    bf = lambda a: a.astype(BF16)
    wi_a, wo_a = _ffn_weights(ffn_a_wi[i], ffn_a_wo[i])
    x1 = _ffn(x2, mods, row_x(tm_x), wi_a, wo_a, ln_g[i, 0], ln_b[i, 0], 0, tm_x)
    c1 = _ffn(c2, mods, row_c, wi_a, wo_a, ln_g[i, 0], ln_b[i, 0], 0, tm_c)

    w_rw, w_gm = bf(w_in[i, :, :rwkv_in]), bf(w_in[i, :, rwkv_in:])
    p_rw_x = _inproj(x1, mods, row_x(tm_x), w_rw, 1, tm_x, rwkv_in // 2)
    p_rw_c = _inproj(c1, mods, row_c, w_rw, 1, tm_c, rwkv_in // 2)

    zpad = lambda a, lo, hi: jnp.pad(a, ((0, 0), (lo, hi), (0, 0)))
    lanes = jnp.arange(GROUP_LANES) // RWKV_HEAD
    bd = (lanes[:, None] == lanes[None, :]).astype(BF16)
    prm = dict(mu=mu_shift[i].reshape(1, rwkv_in), w0=w0[i], w_up=bf(zpad(w_up[i], 0, a_lora)), a0=a0[i],
               a_up=bf(zpad(a_up[i], w_lora, 0)), g_up=bf(g_up[i]), k_k=k_k[i].reshape(1, rw),
               k_a=k_a[i].reshape(1, rw))
    tm_p = _pick(seq, 256)
    q_x = _prep(p_rw_x, prm, bd, seq, tm_p, gm=(x1, mods, row_x(tm_p), w_gm, 1))
    q_c = _prep(p_rw_c, prm, bd, ctx_len, ctx_len)
    p_gm_x = q_x.pop("p_gm")
    q_x3 = {k: a.reshape(bsz, seq, rw) for k, a in q_x.items()}
    q_c3 = {k: a.reshape(bsz, ctx_len, rw) for k, a in q_c.items()}

    ng = rw // GROUP_LANES
    s_zero = jnp.zeros((bsz, 2, ng, GROUP_LANES, GROUP_LANES), F32)
    (s_ctx,) = _scan(q_c3, s_zero, with_y=False)
    yf, yb, _ = _scan(q_x3, s_ctx, with_y=True)

    out_prm = dict(r_k=r_k[i].reshape(1, rw), gn_g=gn_g[i].reshape(1, rw), gn_b=gn_b[i].reshape(1, rw),
                   gm_ln_g=gm_ln_g[i].reshape(1, rw), gm_ln_b=gm_ln_b[i].reshape(1, rw), gm_ws=bf(gm_ws[i]),
                   gm_bs_full=jnp.repeat(gm_bs[i].T, GMLP_GROUP, axis=1), w_out=bf(w_out[i]),
                   ln_g=ln_g[i, 1].reshape(1, d), ln_b=ln_b[i, 1].reshape(1, d))
    tm_o = _pick(seq, 256)
    x2b = _mixout(yf.reshape(bsz * seq, rw), yb.reshape(bsz * seq, rw), q_x, p_gm_x, x1, mods, row_x(tm_o),
                  out_prm, bd, 1, tm_o)

    wi_b, wo_b = _ffn_weights(ffn_b_wi[i], ffn_b_wo[i])
    x3 = _ffn(x2b, mods, row_x(tm_x), wi_b, wo_b, ln_g[i, 2], ln_b[i, 2], 2, tm_x)
    return x3.reshape(bsz, seq, d)
```

```python
import functools
import math

import jax
import jax.numpy as jnp
from jax.experimental import pallas as pl
from jax.experimental.pallas import tpu as pltpu

F32 = jnp.float32
BF16 = jnp.bfloat16

GRID_W = 64
RWKV_HEAD = 64
CHUNK = 128
GMLP_GROUP = 64
N_MOD = 9
LN_EPS = 1e-5
GN_EPS = 64e-5
DEPTH = 1
ALPHA = (2.0 * DEPTH) ** 0.25

SCAN_CHUNK = 64
HEADS_PER_GROUP = 4
GROUP_LANES = HEADS_PER_GROUP * RWKV_HEAD
SCAN_BATCH = 2
VMEM_LIMIT = 56 * 1024 * 1024


def _cparams(sem):
    return pltpu.CompilerParams(dimension_semantics=sem, vmem_limit_bytes=VMEM_LIMIT)


def _pick(n, pref):
    t = min(n, pref)
    while n % t:
        t -= 64
    return t


def _sigmoid(z):
    return 1.0 / (1.0 + jnp.exp(-z))


def _silu(z):
    return z * _sigmoid(z)


def _gelu_tanh(z):
    return 0.5 * z * (1.0 + jnp.tanh(math.sqrt(2.0 / math.pi) * (z + 0.044715 * (z * z * z))))


def _layer_norm(z, g, b):
    mu = jnp.mean(z, axis=-1, keepdims=True)
    d = z - mu
    var = jnp.mean(d * d, axis=-1, keepdims=True)
    return d * jax.lax.rsqrt(var + LN_EPS) * g + b


def _dot(a, b):
    return jnp.dot(a, b, preferred_element_type=F32)


def _dot_nt(a, b):
    return jax.lax.dot_general(a, b, (((1,), (1,)), ((), ())), preferred_element_type=F32)


def _group_sum(z, bd):
    return _dot(z.astype(BF16), bd)


def _head_sum(z, bd):
    w = z.shape[-1]
    return jnp.concatenate([_group_sum(z[:, s:s + GROUP_LANES], bd) for s in range(0, w, GROUP_LANES)], axis=1)


def _ada_kernel(c_ref, w_ref, b_ref, o_ref):
    a = _silu(c_ref[...]).astype(BF16)
    o_ref[...] = _dot(a, w_ref[...].astype(BF16)) + b_ref[...]


def _ada(cc, w_ada, b_ada):
    m, d = cc.shape
    n = w_ada.shape[1]
    tn = 1024
    return pl.pallas_call(
        _ada_kernel,
        grid=(n // tn,),
        in_specs=[pl.BlockSpec((m, d), lambda j: (0, 0)),
                  pl.BlockSpec((d, tn), lambda j: (0, j)),
                  pl.BlockSpec((1, tn), lambda j: (0, j))],
        out_specs=pl.BlockSpec((m, tn), lambda j: (0, j)),
        out_shape=jax.ShapeDtypeStruct((m, n), F32),
        compiler_params=_cparams(("arbitrary",)),
        name="ada",
    )(cc, w_ada, b_ada.reshape(1, n))


def _ffn_kernel(x_ref, mod_ref, wig_ref, wiu_ref, wo_ref, g_ref, b_ref, o_ref, h_sc, acc_sc, *, slot):
    f = pl.program_id(1)

    @pl.when(f == 0)
    def _():
        shift = mod_ref[0, 3 * slot:3 * slot + 1, :]
        scale = mod_ref[0, 3 * slot + 1:3 * slot + 2, :]
        h_sc[...] = (x_ref[...] * (1.0 + scale) + shift).astype(BF16)
        acc_sc[...] = jnp.zeros_like(acc_sc)

    h = h_sc[...]
    act = (_silu(_dot(h, wig_ref[...])) * _dot(h, wiu_ref[...])).astype(BF16)
    acc_sc[...] += _dot(act, wo_ref[...])

    @pl.when(f == pl.num_programs(1) - 1)
    def _():
        gmod = mod_ref[0, 3 * slot + 2:3 * slot + 3, :]
        z = ALPHA * x_ref[...] + 0.5 * gmod * acc_sc[...]
        o_ref[...] = _layer_norm(z, g_ref[...], b_ref[...])


FFN_TF = 512


def _ffn_weights(wi, wo):
    return wi.astype(BF16), wo.astype(BF16)


def _ffn(x2, mods, mod_row, wi, wo, ln_g, ln_b, slot, tm):
    ntok, d = x2.shape
    dff = wo.shape[0]
    tf = FFN_TF
    nf = dff // tf
    return pl.pallas_call(
        functools.partial(_ffn_kernel, slot=slot),
        grid=(ntok // tm, nf),
        in_specs=[pl.BlockSpec((tm, d), lambda i, f: (i, 0)),
                  pl.BlockSpec((1, N_MOD, d), lambda i, f: (mod_row(i), 0, 0)),
                  pl.BlockSpec((d, tf), lambda i, f: (0, f)),
                  pl.BlockSpec((d, tf), lambda i, f: (0, nf + f)),
                  pl.BlockSpec((tf, d), lambda i, f: (f, 0)),
                  pl.BlockSpec((1, d), lambda i, f: (0, 0)),
                  pl.BlockSpec((1, d), lambda i, f: (0, 0))],
        out_specs=pl.BlockSpec((tm, d), lambda i, f: (i, 0)),
        out_shape=jax.ShapeDtypeStruct((ntok, d), F32),
        scratch_shapes=[pltpu.VMEM((tm, d), BF16), pltpu.VMEM((tm, d), F32)],
        compiler_params=_cparams(("parallel", "arbitrary")),
        name=f"ffn{slot}",
    )(x2, mods, wi, wi, wo, ln_g.reshape(1, d), ln_b.reshape(1, d))


def _inproj_kernel(x_ref, mod_ref, w_ref, o_ref, *, slot):
    shift = mod_ref[0, 3 * slot:3 * slot + 1, :]
    scale = mod_ref[0, 3 * slot + 1:3 * slot + 2, :]
    h = (x_ref[...] * (1.0 + scale) + shift).astype(BF16)
    o_ref[...] = _dot(h, w_ref[...])


def _inproj(x2, mods, mod_row, w, slot, tm, tn):
    ntok, d = x2.shape
    n = w.shape[1]
    return pl.pallas_call(
        functools.partial(_inproj_kernel, slot=slot),
        grid=(n // tn, ntok // tm),
        in_specs=[pl.BlockSpec((tm, d), lambda j, i: (i, 0)),
                  pl.BlockSpec((1, N_MOD, d), lambda j, i: (mod_row(i), 0, 0)),
                  pl.BlockSpec((d, tn), lambda j, i: (0, j))],
        out_specs=pl.BlockSpec((tm, tn), lambda j, i: (i, j)),
        out_shape=jax.ShapeDtypeStruct((ntok, n), F32),
        compiler_params=_cparams(("parallel", "parallel")),
        name="inproj",
    )(x2, mods, w)


_PREP_NAMES = ("r", "v", "g", "kk", "kd0", "kd1", "ic0", "ic1", "cum0", "cum1")
_PREP_DTYPES = (BF16,) * 8 + (F32,) * 2


def _prep_kernel(*refs, grid_mode, tiles_per_seq, rw, slot):
    if grid_mode:
        p_ref, up_ref, dn_ref, x_ref, mod_ref, wgm_ref = refs[:6]
        refs = refs[6:]
    else:
        p_ref = refs[0]
        refs = refs[1:]
    (mu_ref, w0_ref, wup_ref, a0_ref, aup_ref, gup_ref, kk_ref, ka_ref, bd_ref, tri0_ref, tri1_ref,
     r_o, v_o, g_o, kk_o, kd0_o, kd1_o, ic0_o, ic1_o, cum0_o, cum1_o) = refs[:21]
    tm = p_ref.shape[0]
    n_gm = 4

    def gm_chunk(j):
        if grid_mode:
            pgm_o = refs[21]
            wn = pgm_o.shape[1] // n_gm
            pgm_o[:, j * wn:(j + 1) * wn] = _dot(h_gm, wgm_ref[:, j * wn:(j + 1) * wn]).astype(BF16)

    def mixed(lo, hi):
        p = p_ref[:, lo:hi]
        width = hi - lo
        row = jax.lax.broadcasted_iota(jnp.int32, (tm, width), 0)
        lane = jax.lax.broadcasted_iota(jnp.int32, (tm, width), 1)
        prev = pltpu.roll(p, 1, axis=0)
        nxt = pltpu.roll(p, tm - 1, axis=0)
        if grid_mode:
            i = pl.program_id(0) % tiles_per_seq
            col = row % GRID_W
            left = jnp.where(col > 0, prev, 0.0)
            right = jnp.where(col < GRID_W - 1, nxt, 0.0)
            up_halo = jnp.where(i > 0, up_ref[:, lo:hi], 0.0)
            dn_halo = jnp.where(i < tiles_per_seq - 1, dn_ref[:, lo:hi], 0.0)
            up = jnp.concatenate([up_halo, p[:tm - GRID_W]], axis=0)
            down = jnp.concatenate([p[GRID_W:], dn_halo], axis=0)
            c4 = lane % 4
            shifted = jnp.where(c4 == 0, left, jnp.where(c4 == 1, right, jnp.where(c4 == 2, up, down)))
        else:
            prev = jnp.where(row > 0, prev, 0.0)
            nxt = jnp.where(row < tm - 1, nxt, 0.0)
            shifted = jnp.where(lane % 2 == 0, prev, nxt)
        return p + (shifted - p) * mu_ref[:, lo:hi]

    if grid_mode:
        shift = mod_ref[0, 3 * slot:3 * slot + 1, :]
        scale = mod_ref[0, 3 * slot + 1:3 * slot + 2, :]
        h_gm = (x_ref[...] * (1.0 + scale) + shift).astype(BF16)
    wa_lo = mixed(3 * rw, 3 * rw + 128)
    g_lo = mixed(3 * rw + 128, 3 * rw + 256)
    tanh_wa = jnp.tanh(wa_lo).astype(BF16)
    wa_bf = wa_lo.astype(BF16)
    sig_g = _sigmoid(g_lo).astype(BF16)
    gm_chunk(0)
    k = mixed(rw, 2 * rw)
    kk = k * kk_ref[...]
    kk_sq = kk * kk
    g_lin = _dot(sig_g, gup_ref[...])
    z_w = [_dot(tanh_wa, wup_ref[d]) for d in range(2)]
    z_a = [_dot(wa_bf, aup_ref[d]) for d in range(2)]
    gm_chunk(1)
    ss = _head_sum(kk_sq, bd_ref[...])
    gm_chunk(2)
    g_o[...] = g_lin.astype(BF16)
    r_o[...] = mixed(0, rw).astype(BF16)
    v_o[...] = mixed(2 * rw, 3 * rw).astype(BF16)
    kk_o[...] = (kk / jnp.maximum(jnp.sqrt(ss), 1e-12)).astype(BF16)
    for d, (tri_ref, cum_o, kd_o, ic_o) in enumerate(((tri0_ref, cum0_o, kd0_o, ic0_o),
                                                      (tri1_ref, cum1_o, kd1_o, ic1_o))):
        lw = -math.exp(-0.5) * _sigmoid(w0_ref[d:d + 1, :] + z_w[d])
        hi = lw.astype(BF16)
        rem = lw - hi.astype(F32)
        mid = rem.astype(BF16)
        lo = (rem - mid.astype(F32)).astype(BF16)
        tri = tri_ref[...]
        cum_o[...] = _dot(tri, hi) + _dot(tri, mid) + _dot(tri, lo)
        iclr = _sigmoid(a0_ref[d:d + 1, :] + z_a[d])
        ic_o[...] = iclr.astype(BF16)
        kd_o[...] = (k * (1.0 + (iclr - 1.0) * ka_ref[...])).astype(BF16)
    gm_chunk(3)


def _prep(p_rw, params, bd, tokens_per_seq, tm, gm=None):
    ntok, width = p_rw.shape
    rw = params["w0"].shape[1]
    grid_mode = gm is not None
    tiles_per_seq = tokens_per_seq // tm
    hb = tm // GRID_W
    nhalo = ntok // GRID_W
    const = lambda a: pl.BlockSpec(a.shape, lambda i: (0,) * a.ndim, pipeline_mode=pl.Buffered(1))
    tok = lambda w: pl.BlockSpec((tm, w), lambda i: (i, 0))
    in_specs = [tok(width)]
    args = [p_rw]
    names, dtypes, widths = list(_PREP_NAMES), list(_PREP_DTYPES), [rw] * 10
    slot = 0
    if grid_mode:
        x2, mods, mod_row, w_gm, slot = gm
        d = x2.shape[1]
        in_specs += [pl.BlockSpec((GRID_W, width), lambda i: (jnp.maximum(i * hb - 1, 0), 0)),
                     pl.BlockSpec((GRID_W, width), lambda i: (jnp.minimum((i + 1) * hb, nhalo - 1), 0)),
                     tok(d), pl.BlockSpec((1, N_MOD, d), lambda i: (mod_row(i), 0, 0)), const(w_gm)]
        args += [p_rw, p_rw, x2, mods, w_gm]
        names.append("p_gm")
        dtypes.append(BF16)
        widths.append(w_gm.shape[1])
    ti = jnp.arange(tm)[:, None]
    tj = jnp.arange(tm)[None, :]
    same = (ti // SCAN_CHUNK) == (tj // SCAN_CHUNK)
    tri = [(same & (tj <= ti)).astype(BF16), (same & (tj >= ti)).astype(BF16)]
    plist = [params[k] for k in ("mu", "w0", "w_up", "a0", "a_up", "g_up", "k_k", "k_a")] + [bd] + tri
    in_specs += [const(a) for a in plist]
    outs = pl.pallas_call(
        functools.partial(_prep_kernel, grid_mode=grid_mode, tiles_per_seq=tiles_per_seq, rw=rw, slot=slot),
        grid=(ntok // tm,),
        in_specs=in_specs,
        out_specs=[tok(w) for w in widths],
        out_shape=[jax.ShapeDtypeStruct((ntok, w), dt) for w, dt in zip(widths, dtypes)],
        compiler_params=_cparams(("parallel",)),
        name="prep_grid" if grid_mode else "prep_seq",
    )(*args, *plist)
    return dict(zip(names, outs))


def _stackmask(z, lane_head):
    return jnp.concatenate([jnp.where(lane_head == h, z, jnp.zeros_like(z)) for h in range(HEADS_PER_GROUP)],
                           axis=0)


def _scan_chunk(units, with_y):
    c, l = SCAN_CHUNK, GROUP_LANES
    gc = HEADS_PER_GROUP * c
    n_lvl = int(math.log2(c))
    t = jax.lax.broadcasted_iota(jnp.int32, (c, gc), 0)
    tj = jax.lax.broadcasted_iota(jnp.int32, (c, gc), 1) % c
    lane_head = jax.lax.broadcasted_iota(jnp.int32, (c, l), 1) // RWKV_HEAD
    bd = functools.partial(_stackmask, lane_head=lane_head)
    eye = jnp.where(tj == t, 1.0, 0.0)

    for w in units:
        w.update(w["load"]())
        w["at"] = (-w["kk"] * w["dexc"]).astype(BF16)
        w["rt"] = (w["r"] * w["dinc"]).astype(BF16)
        bt = (w["b"] * w["dinv"]).astype(BF16)
        kt = (w["kd"] * w["dinv"]).astype(BF16)
        lhs = jnp.concatenate([w["at"], w["rt"]], axis=0) if with_y else w["at"]
        rhs = jnp.concatenate([bd(bt), bd(kt)], axis=0)
        w["a_all"] = _dot_nt(lhs, rhs)
        w["from_s"] = _dot_nt(lhs, w["s"].astype(BF16))
    for w in units:
        strict = (tj > t) if w["reverse"] else (tj < t)
        a_ab = jnp.where(strict, w["a_all"][:c, :gc], 0.0)
        a_v = jnp.where(strict, w["a_all"][:c, gc:], 0.0)
        if with_y:
            incl = (tj >= t) if w["reverse"] else (tj <= t)
            a_v = jnp.concatenate([a_v, jnp.where(incl, w["a_all"][c:, gc:], 0.0)], axis=0)
            w["a_rb"] = jnp.where(incl, w["a_all"][c:, :gc], 0.0).astype(BF16)
        w["from_v"] = _dot(a_v.astype(BF16), bd(w["v"].astype(BF16)))
        w["u"] = w["from_s"][:c] + w["from_v"][:c]
        w["p"] = eye + a_ab
        w["apow"] = a_ab.astype(BF16)
    for w in units:
        w["apow"] = _dot(w["apow"], bd(w["apow"])).astype(BF16)
    for lvl in range(1, n_lvl):
        last = lvl == n_lvl - 1
        for w in units:
            p_bf = w["p"].astype(BF16)
            if last:
                w["p"] = w["p"] + _dot(p_bf, bd(w["apow"]))
            else:
                both = _dot(jnp.concatenate([p_bf, w["apow"]], axis=0), bd(w["apow"]))
                w["p"] = w["p"] + both[:c]
                w["apow"] = both[c:].astype(BF16)
    for w in units:
        w["sa"] = _dot(w["p"].astype(BF16), bd(w["u"].astype(BF16)))
    outs = []
    rh = jax.lax.broadcasted_iota(jnp.int32, (l, l), 0) // RWKV_HEAD
    ch = jax.lax.broadcasted_iota(jnp.int32, (l, l), 1) // RWKV_HEAD
    for w in units:
        y = None
        if with_y:
            y = w["from_s"][c:] + w["from_v"][c:] + _dot(w["a_rb"], bd(w["sa"].astype(BF16)))
        sav_t = jnp.concatenate([w["sa"], w["v"]], axis=0).T.astype(BF16)
        bk = jnp.concatenate([w["b"] * w["dend"], w["kd"] * w["dend"]], axis=0).astype(BF16)
        upd = _dot(sav_t, bk)
        outs.append((w["s"] * w["dtot"] + jnp.where(rh == ch, upd, 0.0), y))
    return outs


def _scan_kernel(*refs, with_y):
    ins = refs[:13]
    s0_ref = ins[12]
    if with_y:
        yf_ref, yb_ref, sfin_ref, s_sc = refs[13:]
        y_refs = (yf_ref, yb_ref)
    else:
        sfin_ref, s_sc = refs[13:]
        y_refs = (None, None)
    ci = pl.program_id(1)

    @pl.when(ci == 0)
    def _():
        s_sc[...] = s0_ref[...]

    units = []
    for n in range(s_sc.shape[0]):
        for d in range(2):
            r_ref, v_ref, kk_ref, cum_ref, kd_ref, ic_ref = ins[6 * d:6 * d + 6]
            reverse = d == 1

            def load(n=n, d=d, reverse=reverse, r_ref=r_ref, v_ref=v_ref, kk_ref=kk_ref, cum_ref=cum_ref,
                     kd_ref=kd_ref, ic_ref=ic_ref, sl=None, g=None):
                cum = cum_ref[n, :, sl]
                c, width = cum.shape
                row = jax.lax.broadcasted_iota(jnp.int32, (c, width), 0)
                if reverse:
                    cum_excl = jnp.where(row == c - 1, 0.0, pltpu.roll(cum, c - 1, axis=0))
                    tot = cum[0:1, :]
                else:
                    cum_excl = jnp.where(row == 0, 0.0, pltpu.roll(cum, 1, axis=0))
                    tot = cum[c - 1:c, :]
                kk = kk_ref[n, :, sl].astype(F32)
                return dict(r=r_ref[n, :, sl].astype(F32), v=v_ref[n, :, sl].astype(F32), kk=kk,
                            kd=kd_ref[n, :, sl].astype(F32), b=kk * ic_ref[n, :, sl].astype(F32),
                            dinc=jnp.exp(cum), dexc=jnp.exp(cum_excl), dinv=jnp.exp(-cum),
                            dend=jnp.exp(tot - cum), dtot=jnp.exp(tot), s=s_sc[n, d, g])

            for g in range(r_ref.shape[-1] // GROUP_LANES):
                sl = slice(g * GROUP_LANES, (g + 1) * GROUP_LANES)
                units.append(dict(load=functools.partial(load, sl=sl, g=g), reverse=reverse, n=n, d=d, g=g, sl=sl))

    for w, (s_new, y) in zip(units, _scan_chunk(units, with_y)):
        s_sc[w["n"], w["d"], w["g"]] = s_new
        if with_y:
            y_refs[w["d"]][w["n"], :, w["sl"]] = y

    @pl.when(ci == pl.num_programs(1) - 1)
    def _():
        sfin_ref[...] = s_sc[...]


def _scan(q, s0, with_y):
    bsz, t, rw = q["r"].shape
    c = SCAN_CHUNK
    nb = SCAN_BATCH if bsz % SCAN_BATCH == 0 else 1
    nc = t // c
    ng = rw // GROUP_LANES
    fwd = pl.BlockSpec((nb, c, rw), lambda b, i: (b, i, 0))
    bwd = pl.BlockSpec((nb, c, rw), lambda b, i: (b, nc - 1 - i, 0))
    s_spec = pl.BlockSpec((nb, 2, ng, GROUP_LANES, GROUP_LANES), lambda b, i: (b, 0, 0, 0, 0))
    args = [q["r"], q["v"], q["kk"], q["cum0"], q["kd0"], q["ic0"],
            q["r"], q["v"], q["kk"], q["cum1"], q["kd1"], q["ic1"], s0]
    in_specs = [fwd] * 6 + [bwd] * 6 + [s_spec]
    y_shape = jax.ShapeDtypeStruct((bsz, t, rw), F32)
    out_specs = ([fwd, bwd] if with_y else []) + [s_spec]
    out_shape = ([y_shape, y_shape] if with_y else []) + [jax.ShapeDtypeStruct(s0.shape, F32)]
    outs = pl.pallas_call(
        functools.partial(_scan_kernel, with_y=with_y),
        grid=(bsz // nb, nc),
        in_specs=in_specs,
        out_specs=out_specs,
        out_shape=out_shape,
        scratch_shapes=[pltpu.VMEM((nb, 2, ng, GROUP_LANES, GROUP_LANES), F32)],
        compiler_params=_cparams(("parallel", "arbitrary")),
        name="scan_y" if with_y else "scan_state",
    )(*args)
    return outs


def _mixout_kernel(yf_ref, yb_ref, r_ref, v_ref, g_ref, kd0_ref, kd1_ref, gm_ref, x_ref, mod_ref,
                   rk_ref, gng_ref, gnb_ref, glg_ref, glb_ref, ws_ref, bs_ref, wout_ref, lng_ref, lnb_ref,
                   bd_ref, o_ref, *, slot):
    bd = bd_ref[...]
    rw = yf_ref.shape[-1]
    inv_n = 1.0 / RWKV_HEAD
    inv_g = 1.0 / GMLP_GROUP
    f32 = lambda ref: ref[...].astype(F32)
    y = yf_ref[...] + yb_ref[...]
    vv = _gelu_tanh(gm_ref[:, rw:].astype(F32))
    mu = _head_sum(y, bd) * inv_n
    mu_v = _head_sum(vv, bd) * inv_g
    bonus_in = f32(r_ref) * (f32(kd0_ref) + f32(kd1_ref)) * rk_ref[...]
    dy = y - mu
    var = _head_sum(dy * dy, bd) * inv_n
    dv = vv - mu_v
    var_v = _head_sum(dv * dv, bd) * inv_g
    bonus = _head_sum(bonus_in, bd) * f32(v_ref)
    yn = dy * jax.lax.rsqrt(var + GN_EPS) * gng_ref[...] + gnb_ref[...]
    out_r = ((yn + bonus) * f32(g_ref)).astype(BF16)
    vn = (dv * jax.lax.rsqrt(var_v + LN_EPS) * glg_ref[...] + glb_ref[...]).astype(BF16)
    o_r = _dot(out_r, wout_ref[0:rw, :])
    u = _gelu_tanh(gm_ref[:, :rw].astype(F32))
    tm = vn.shape[0]
    lane = jax.lax.broadcasted_iota(jnp.int32, (CHUNK, 2 * GMLP_GROUP), 1)
    chunks = []
    for n in range(tm // CHUNK):
        pairs = []
        for gp in range(rw // (2 * GMLP_GROUP)):
            v2 = vn[n * CHUNK:(n + 1) * CHUNK, gp * 2 * GMLP_GROUP:(gp + 1) * 2 * GMLP_GROUP]
            m0 = _dot(ws_ref[2 * gp], v2)
            m1 = _dot(ws_ref[2 * gp + 1], v2)
            pairs.append(jnp.where(lane < GMLP_GROUP, m0, m1))
        chunks.append(jnp.concatenate(pairs, axis=1) + bs_ref[...])
    mixed = jnp.concatenate(chunks, axis=0)
    out_g = (u * mixed).astype(BF16)

    o = o_r + _dot(out_g, wout_ref[rw:, :])
    gate = mod_ref[0, 3 * slot + 2:3 * slot + 3, :]
    z = ALPHA * x_ref[...] + gate * o
    o_ref[...] = _layer_norm(z, lng_ref[...], lnb_ref[...])


def _mixout(yf, yb, q, p_gm, x2, mods, mod_row, params, bd, slot, tm):
    ntok, d = x2.shape
    rw = yf.shape[-1]
    tok = lambda w: pl.BlockSpec((tm, w), lambda i: (i, 0))
    const = lambda a: pl.BlockSpec(a.shape, lambda i: (0,) * a.ndim, pipeline_mode=pl.Buffered(1))
    plist = [params[k] for k in ("r_k", "gn_g", "gn_b", "gm_ln_g", "gm_ln_b", "gm_ws", "gm_bs_full", "w_out",
                                 "ln_g", "ln_b")] + [bd]
    return pl.pallas_call(
        functools.partial(_mixout_kernel, slot=slot),
        grid=(ntok // tm,),
        in_specs=[tok(rw)] * 7 + [tok(p_gm.shape[1]), tok(d),
                                  pl.BlockSpec((1, N_MOD, d), lambda i: (mod_row(i), 0, 0))]
                 + [const(a) for a in plist],
        out_specs=tok(d),
        out_shape=jax.ShapeDtypeStruct((ntok, d), F32),
        compiler_params=_cparams(("parallel",)),
        name="mixout",
    )(yf, yb, q["r"], q["v"], q["g"], q["kd0"], q["kd1"], p_gm, x2, mods, *plist)


def kernel(x, c, ctx, c_ctx, w_ada, b_ada, ln_g, ln_b, ffn_a_wi, ffn_a_wo, ffn_b_wi, ffn_b_wo, w_in, mu_shift,
           w0, w_up, a0, a_up, g_up, k_k, k_a, r_k, gn_g, gn_b, gm_ln_g, gm_ln_b, gm_ws, gm_bs, w_out):
    bsz, seq, d = x.shape
    ctx_len = ctx.shape[1]
    assert w_ada.shape[0] == DEPTH
    rw = w0.shape[-1]
    rwkv_in = mu_shift.shape[-1]
    w_lora, a_lora = w_up.shape[2], a_up.shape[2]
    assert w_lora + a_lora == 128 and g_up.shape[1] == 128 and rwkv_in == 3 * rw + 256
    assert seq % GRID_W == 0 and seq % SCAN_CHUNK == 0 and ctx_len % SCAN_CHUNK == 0 and seq % CHUNK == 0
    i = 0

    n_rows = -(-(bsz + 1) // 8) * 8
    cc = jnp.zeros((n_rows, d), F32).at[:bsz].set(c).at[bsz].set(c_ctx)
    mods = _ada(cc, w_ada[i], b_ada[i]).reshape(n_rows, N_MOD, d)

    x2 = x.reshape(bsz * seq, d)
    c2 = ctx.reshape(bsz * ctx_len, d)
    tm_x = _pick(seq, 512)
    tm_c = _pick(ctx_len, 512)
    row_x = lambda tm: (lambda t: t // (seq // tm))
    row_c = lambda t: bsz

    bf = lambda a: a.astype(BF16)
    wi_a, wo_a = _ffn_weights(ffn_a_wi[i], ffn_a_wo[i])
    x1 = _ffn(x2, mods, row_x(tm_x), wi_a, wo_a, ln_g[i, 0], ln_b[i, 0], 0, tm_x)
    c1 = _ffn(c2, mods, row_c, wi_a, wo_a, ln_g[i, 0], ln_b[i, 0], 0, tm_c)

    w_rw, w_gm = bf(w_in[i, :, :rwkv_in]), bf(w_in[i, :, rwkv_in:])
    p_rw_x = _inproj(x1, mods, row_x(tm_x), w_rw, 1, tm_x, rwkv_in // 2)
    p_rw_c = _inproj(c1, mods, row_c, w_rw, 1, tm_c, rwkv_in // 2)

    zpad = lambda a, lo, hi: jnp.pad(a, ((0, 0), (lo, hi), (0, 0)))
    lanes = jnp.arange(GROUP_LANES) // RWKV_HEAD
    bd = (lanes[:, None] == lanes[None, :]).astype(BF16)
    prm = dict(mu=mu_shift[i].reshape(1, rwkv_in), w0=w0[i], w_up=bf(zpad(w_up[i], 0, a_lora)), a0=a0[i],
               a_up=bf(zpad(a_up[i], w_lora, 0)), g_up=bf(g_up[i]), k_k=k_k[i].reshape(1, rw),
               k_a=k_a[i].reshape(1, rw))
    tm_p = _pick(seq, 256)
    q_x = _prep(p_rw_x, prm, bd, seq, tm_p, gm=(x1, mods, row_x(tm_p), w_gm, 1))
    q_c = _prep(p_rw_c, prm, bd, ctx_len, ctx_len)
    p_gm_x = q_x.pop("p_gm")
    q_x3 = {k: a.reshape(bsz, seq, rw) for k, a in q_x.items()}
    q_c3 = {k: a.reshape(bsz, ctx_len, rw) for k, a in q_c.items()}

    ng = rw // GROUP_LANES
    s_zero = jnp.zeros((bsz, 2, ng, GROUP_LANES, GROUP_LANES), F32)
    (s_ctx,) = _scan(q_c3, s_zero, with_y=False)
    yf, yb, _ = _scan(q_x3, s_ctx, with_y=True)

    out_prm = dict(r_k=r_k[i].reshape(1, rw), gn_g=gn_g[i].reshape(1, rw), gn_b=gn_b[i].reshape(1, rw),
                   gm_ln_g=gm_ln_g[i].reshape(1, rw), gm_ln_b=gm_ln_b[i].reshape(1, rw), gm_ws=bf(gm_ws[i]),
                   gm_bs_full=jnp.repeat(gm_bs[i].T, GMLP_GROUP, axis=1), w_out=bf(w_out[i]),
                   ln_g=ln_g[i, 1].reshape(1, d), ln_b=ln_b[i, 1].reshape(1, d))
    tm_o = _pick(seq, 256)
    x2b = _mixout(yf.reshape(bsz * seq, rw), yb.reshape(bsz * seq, rw), q_x, p_gm_x, x1, mods, row_x(tm_o),
                  out_prm, bd, 1, tm_o)

    wi_b, wo_b = _ffn_weights(ffn_b_wi[i], ffn_b_wo[i])
    x3 = _ffn(x2b, mods, row_x(tm_x), wi_b, wo_b, ln_g[i, 2], ln_b[i, 2], 2, tm_x)
    return x3.reshape(bsz, seq, d)
```

```python
import functools
import math

import jax
import jax.numpy as jnp
from jax.experimental import pallas as pl
from jax.experimental.pallas import tpu as pltpu

F32 = jnp.float32
BF16 = jnp.bfloat16

GRID_W = 64
RWKV_HEAD = 64
CHUNK = 128
GMLP_GROUP = 64
N_MOD = 9
LN_EPS = 1e-5
GN_EPS = 64e-5
DEPTH = 1
ALPHA = (2.0 * DEPTH) ** 0.25

SCAN_CHUNK = 64
HEADS_PER_GROUP = 4
GROUP_LANES = HEADS_PER_GROUP * RWKV_HEAD
SCAN_BATCH = 2
SCAN_SUB = 2
VMEM_LIMIT = 56 * 1024 * 1024


def _cparams(sem):
    return pltpu.CompilerParams(dimension_semantics=sem, vmem_limit_bytes=VMEM_LIMIT)


def _pick(n, pref):
    t = min(n, pref)
    while n % t:
        t -= 64
    return t


def _sigmoid(z):
    return 1.0 / (1.0 + jnp.exp(-z))


def _silu(z):
    return z * _sigmoid(z)


def _gelu_tanh(z):
    return 0.5 * z * (1.0 + jnp.tanh(math.sqrt(2.0 / math.pi) * (z + 0.044715 * (z * z * z))))


def _layer_norm(z, g, b):
    mu = jnp.mean(z, axis=-1, keepdims=True)
    d = z - mu
    var = jnp.mean(d * d, axis=-1, keepdims=True)
    return d * jax.lax.rsqrt(var + LN_EPS) * g + b


def _dot(a, b):
    return jnp.dot(a, b, preferred_element_type=F32)


def _dot_nt(a, b):
    return jax.lax.dot_general(a, b, (((1,), (1,)), ((), ())), preferred_element_type=F32)


def _group_sum(z, bd):
    return _dot(z.astype(BF16), bd)


def _head_sum(z, bd):
    w = z.shape[-1]
    return jnp.concatenate([_group_sum(z[:, s:s + GROUP_LANES], bd) for s in range(0, w, GROUP_LANES)], axis=1)


def _ada_kernel(c_ref, w_ref, b_ref, o_ref):
    a = _silu(c_ref[...]).astype(BF16)
    o_ref[...] = _dot(a, w_ref[...].astype(BF16)) + b_ref[...]


def _ada(cc, w_ada, b_ada):
    m, d = cc.shape
    n = w_ada.shape[1]
    tn = 1024
    return pl.pallas_call(
        _ada_kernel,
        grid=(n // tn,),
        in_specs=[pl.BlockSpec((m, d), lambda j: (0, 0)),
                  pl.BlockSpec((d, tn), lambda j: (0, j)),
                  pl.BlockSpec((1, tn), lambda j: (0, j))],
        out_specs=pl.BlockSpec((m, tn), lambda j: (0, j)),
        out_shape=jax.ShapeDtypeStruct((m, n), F32),
        compiler_params=_cparams(("arbitrary",)),
        name="ada",
    )(cc, w_ada, b_ada.reshape(1, n))


def _ffn_kernel(x_ref, mod_ref, wig_ref, wiu_ref, wo_ref, g_ref, b_ref, o_ref, h_sc, acc_sc, *, slot):
    f = pl.program_id(1)

    @pl.when(f == 0)
    def _():
        shift = mod_ref[0, 3 * slot:3 * slot + 1, :]
        scale = mod_ref[0, 3 * slot + 1:3 * slot + 2, :]
        h_sc[...] = (x_ref[...] * (1.0 + scale) + shift).astype(BF16)
        acc_sc[...] = jnp.zeros_like(acc_sc)

    h = h_sc[...]
    half = wo_ref.shape[0] // 2
    cols = [slice(j * half, (j + 1) * half) for j in range(2)]
    gate_up = [(_dot(h, wig_ref[:, c]), _dot(h, wiu_ref[:, c])) for c in cols]
    out = None
    for c, (gate, up) in zip(cols, gate_up):
        part = _dot((_silu(gate) * up).astype(BF16), wo_ref[c, :])
        out = part if out is None else out + part
    acc_sc[...] += out

    @pl.when(f == pl.num_programs(1) - 1)
    def _():
        gmod = mod_ref[0, 3 * slot + 2:3 * slot + 3, :]
        z = ALPHA * x_ref[...] + 0.5 * gmod * acc_sc[...]
        o_ref[...] = _layer_norm(z, g_ref[...], b_ref[...])


FFN_TF = 512


def _ffn_weights(wi, wo):
    return wi.astype(BF16), wo.astype(BF16)


def _ffn(x2, mods, mod_row, wi, wo, ln_g, ln_b, slot, tm):
    ntok, d = x2.shape
    dff = wo.shape[0]
    tf = FFN_TF
    nf = dff // tf
    return pl.pallas_call(
        functools.partial(_ffn_kernel, slot=slot),
        grid=(ntok // tm, nf),
        in_specs=[pl.BlockSpec((tm, d), lambda i, f: (i, 0)),
                  pl.BlockSpec((1, N_MOD, d), lambda i, f: (mod_row(i), 0, 0)),
                  pl.BlockSpec((d, tf), lambda i, f: (0, f)),
                  pl.BlockSpec((d, tf), lambda i, f: (0, nf + f)),
                  pl.BlockSpec((tf, d), lambda i, f: (f, 0)),
                  pl.BlockSpec((1, d), lambda i, f: (0, 0)),
                  pl.BlockSpec((1, d), lambda i, f: (0, 0))],
        out_specs=pl.BlockSpec((tm, d), lambda i, f: (i, 0)),
        out_shape=jax.ShapeDtypeStruct((ntok, d), F32),
        scratch_shapes=[pltpu.VMEM((tm, d), BF16), pltpu.VMEM((tm, d), F32)],
        compiler_params=_cparams(("parallel", "arbitrary")),
        name=f"ffn{slot}",
    )(x2, mods, wi, wi, wo, ln_g.reshape(1, d), ln_b.reshape(1, d))


def _inproj_kernel(x_ref, mod_ref, w_ref, o_ref, *, slot):
    shift = mod_ref[0, 3 * slot:3 * slot + 1, :]
    scale = mod_ref[0, 3 * slot + 1:3 * slot + 2, :]
    h = (x_ref[...] * (1.0 + scale) + shift).astype(BF16)
    o_ref[...] = _dot(h, w_ref[...])


def _inproj(x2, mods, mod_row, w, slot, tm, tn):
    ntok, d = x2.shape
    n = w.shape[1]
    return pl.pallas_call(
        functools.partial(_inproj_kernel, slot=slot),
        grid=(n // tn, ntok // tm),
        in_specs=[pl.BlockSpec((tm, d), lambda j, i: (i, 0)),
                  pl.BlockSpec((1, N_MOD, d), lambda j, i: (mod_row(i), 0, 0)),
                  pl.BlockSpec((d, tn), lambda j, i: (0, j))],
        out_specs=pl.BlockSpec((tm, tn), lambda j, i: (i, j)),
        out_shape=jax.ShapeDtypeStruct((ntok, n), F32),
        compiler_params=_cparams(("parallel", "parallel")),
        name="inproj",
    )(x2, mods, w)


_PREP_NAMES = ("r", "v", "g", "kk", "kd0", "kd1", "ic0", "ic1", "cum0", "cum1")
_PREP_DTYPES = (BF16,) * 8 + (F32,) * 2


def _prep_kernel(*refs, grid_mode, tiles_per_seq, rw, slot):
    if grid_mode:
        p_ref, up_ref, dn_ref, x_ref, mod_ref, wgm_ref = refs[:6]
        refs = refs[6:]
    else:
        p_ref = refs[0]
        refs = refs[1:]
    (mu_ref, w0_ref, wup_ref, a0_ref, aup_ref, gup_ref, kk_ref, ka_ref, bd_ref, tri0_ref, tri1_ref,
     r_o, v_o, g_o, kk_o, kd0_o, kd1_o, ic0_o, ic1_o, cum0_o, cum1_o) = refs[:21]
    tm = p_ref.shape[0]
    n_gm = 4

    def gm_chunk(j):
        if grid_mode:
            pgm_o = refs[21]
            wn = pgm_o.shape[1] // n_gm
            pgm_o[:, j * wn:(j + 1) * wn] = _dot(h_gm, wgm_ref[:, j * wn:(j + 1) * wn]).astype(BF16)

    def mixed(lo, hi):
        p = p_ref[:, lo:hi]
        width = hi - lo
        row = jax.lax.broadcasted_iota(jnp.int32, (tm, width), 0)
        lane = jax.lax.broadcasted_iota(jnp.int32, (tm, width), 1)
        prev = pltpu.roll(p, 1, axis=0)
        nxt = pltpu.roll(p, tm - 1, axis=0)
        if grid_mode:
            i = pl.program_id(0) % tiles_per_seq
            col = row % GRID_W
            left = jnp.where(col > 0, prev, 0.0)
            right = jnp.where(col < GRID_W - 1, nxt, 0.0)
            up_halo = jnp.where(i > 0, up_ref[:, lo:hi], 0.0)
            dn_halo = jnp.where(i < tiles_per_seq - 1, dn_ref[:, lo:hi], 0.0)
            up = jnp.concatenate([up_halo, p[:tm - GRID_W]], axis=0)
            down = jnp.concatenate([p[GRID_W:], dn_halo], axis=0)
            c4 = lane % 4
            shifted = jnp.where(c4 == 0, left, jnp.where(c4 == 1, right, jnp.where(c4 == 2, up, down)))
        else:
            prev = jnp.where(row > 0, prev, 0.0)
            nxt = jnp.where(row < tm - 1, nxt, 0.0)
            shifted = jnp.where(lane % 2 == 0, prev, nxt)
        return p + (shifted - p) * mu_ref[:, lo:hi]

    if grid_mode:
        shift = mod_ref[0, 3 * slot:3 * slot + 1, :]
        scale = mod_ref[0, 3 * slot + 1:3 * slot + 2, :]
        h_gm = (x_ref[...] * (1.0 + scale) + shift).astype(BF16)
    wa_lo = mixed(3 * rw, 3 * rw + 128)
    g_lo = mixed(3 * rw + 128, 3 * rw + 256)
    tanh_wa = jnp.tanh(wa_lo).astype(BF16)
    wa_bf = wa_lo.astype(BF16)
    sig_g = _sigmoid(g_lo).astype(BF16)
    gm_chunk(0)
    k = mixed(rw, 2 * rw)
    kk = k * kk_ref[...]
    kk_sq = kk * kk
    g_lin = _dot(sig_g, gup_ref[...])
    z_w = [_dot(tanh_wa, wup_ref[d]) for d in range(2)]
    z_a = [_dot(wa_bf, aup_ref[d]) for d in range(2)]
    gm_chunk(1)
    ss = _head_sum(kk_sq, bd_ref[...])
    gm_chunk(2)
    g_o[...] = g_lin.astype(BF16)
    r_o[...] = mixed(0, rw).astype(BF16)
    v_o[...] = mixed(2 * rw, 3 * rw).astype(BF16)
    kk_o[...] = (kk / jnp.maximum(jnp.sqrt(ss), 1e-12)).astype(BF16)
    for d, (tri_ref, cum_o, kd_o, ic_o) in enumerate(((tri0_ref, cum0_o, kd0_o, ic0_o),
                                                      (tri1_ref, cum1_o, kd1_o, ic1_o))):
        lw = -math.exp(-0.5) * _sigmoid(w0_ref[d:d + 1, :] + z_w[d])
        hi = lw.astype(BF16)
        rem = lw - hi.astype(F32)
        mid = rem.astype(BF16)
        lo = (rem - mid.astype(F32)).astype(BF16)
        tri = tri_ref[...]
        cum_o[...] = _dot(tri, hi) + _dot(tri, mid) + _dot(tri, lo)
        iclr = _sigmoid(a0_ref[d:d + 1, :] + z_a[d])
        ic_o[...] = iclr.astype(BF16)
        kd_o[...] = (k * (1.0 + (iclr - 1.0) * ka_ref[...])).astype(BF16)
    gm_chunk(3)


def _prep(p_rw, params, bd, tokens_per_seq, tm, gm=None):
    ntok, width = p_rw.shape
    rw = params["w0"].shape[1]
    grid_mode = gm is not None
    tiles_per_seq = tokens_per_seq // tm
    hb = tm // GRID_W
    nhalo = ntok // GRID_W
    const = lambda a: pl.BlockSpec(a.shape, lambda i: (0,) * a.ndim, pipeline_mode=pl.Buffered(1))
    tok = lambda w: pl.BlockSpec((tm, w), lambda i: (i, 0))
    in_specs = [tok(width)]
    args = [p_rw]
    names, dtypes, widths = list(_PREP_NAMES), list(_PREP_DTYPES), [rw] * 10
    slot = 0
    if grid_mode:
        x2, mods, mod_row, w_gm, slot = gm
        d = x2.shape[1]
        in_specs += [pl.BlockSpec((GRID_W, width), lambda i: (jnp.maximum(i * hb - 1, 0), 0)),
                     pl.BlockSpec((GRID_W, width), lambda i: (jnp.minimum((i + 1) * hb, nhalo - 1), 0)),
                     tok(d), pl.BlockSpec((1, N_MOD, d), lambda i: (mod_row(i), 0, 0)), const(w_gm)]
        args += [p_rw, p_rw, x2, mods, w_gm]
        names.append("p_gm")
        dtypes.append(BF16)
        widths.append(w_gm.shape[1])
    ti = jnp.arange(tm)[:, None]
    tj = jnp.arange(tm)[None, :]
    same = (ti // SCAN_CHUNK) == (tj // SCAN_CHUNK)
    tri = [(same & (tj <= ti)).astype(BF16), (same & (tj >= ti)).astype(BF16)]
    plist = [params[k] for k in ("mu", "w0", "w_up", "a0", "a_up", "g_up", "k_k", "k_a")] + [bd] + tri
    in_specs += [const(a) for a in plist]
    outs = pl.pallas_call(
        functools.partial(_prep_kernel, grid_mode=grid_mode, tiles_per_seq=tiles_per_seq, rw=rw, slot=slot),
        grid=(ntok // tm,),
        in_specs=in_specs,
        out_specs=[tok(w) for w in widths],
        out_shape=[jax.ShapeDtypeStruct((ntok, w), dt) for w, dt in zip(widths, dtypes)],
        compiler_params=_cparams(("parallel",)),
        name="prep_grid" if grid_mode else "prep_seq",
    )(*args, *plist)
    return dict(zip(names, outs))


def _stackmask(z, lane_head):
    return jnp.concatenate([jnp.where(lane_head == h, z, jnp.zeros_like(z)) for h in range(HEADS_PER_GROUP)],
                           axis=0)


def _scan_prepare(units, with_y):
    c, l = SCAN_CHUNK, GROUP_LANES
    gc = HEADS_PER_GROUP * c
    n_lvl = int(math.log2(c))
    t = jax.lax.broadcasted_iota(jnp.int32, (c, gc), 0)
    tj = jax.lax.broadcasted_iota(jnp.int32, (c, gc), 1) % c
    lane_head = jax.lax.broadcasted_iota(jnp.int32, (c, l), 1) // RWKV_HEAD
    bd = functools.partial(_stackmask, lane_head=lane_head)
    eye = jnp.where(tj == t, 1.0, 0.0)

    for w in units:
        w.update(w["load"]())
        at = (-w["kk"] * w["dexc"]).astype(BF16)
        rt = (w["r"] * w["dinc"]).astype(BF16)
        bt = (w["b"] * w["dinv"]).astype(BF16)
        kt = (w["kd"] * w["dinv"]).astype(BF16)
        w["lhs"] = jnp.concatenate([at, rt], axis=0) if with_y else at
        w["a_all"] = _dot_nt(w["lhs"], jnp.concatenate([bd(bt), bd(kt)], axis=0))
        w["bk"] = jnp.concatenate([w["b"] * w["dend"], w["kd"] * w["dend"]], axis=0).astype(BF16)
    for w in units:
        strict = (tj > t) if w["reverse"] else (tj < t)
        a_ab = jnp.where(strict, w["a_all"][:c, :gc], 0.0)
        a_v = jnp.where(strict, w["a_all"][:c, gc:], 0.0)
        if with_y:
            incl = (tj >= t) if w["reverse"] else (tj <= t)
            a_v = jnp.concatenate([a_v, jnp.where(incl, w["a_all"][c:, gc:], 0.0)], axis=0)
            w["a_rb"] = jnp.where(incl, w["a_all"][c:, :gc], 0.0).astype(BF16)
        w["from_v"] = _dot(a_v.astype(BF16), bd(w["v"].astype(BF16)))
        w["p"] = eye + a_ab
        w["apow"] = a_ab.astype(BF16)
    for w in units:
        w["apow"] = _dot(w["apow"], bd(w["apow"])).astype(BF16)
    for lvl in range(1, n_lvl):
        last = lvl == n_lvl - 1
        for w in units:
            p_bf = w["p"].astype(BF16)
            if last:
                w["p"] = (w["p"] + _dot(p_bf, bd(w["apow"]))).astype(BF16)
            else:
                both = _dot(jnp.concatenate([p_bf, w["apow"]], axis=0), bd(w["apow"]))
                w["p"] = w["p"] + both[:c]
                w["apow"] = both[c:].astype(BF16)


def _scan_apply(units, with_y):
    c, l = SCAN_CHUNK, GROUP_LANES
    lane_head = jax.lax.broadcasted_iota(jnp.int32, (c, l), 1) // RWKV_HEAD
    bd = functools.partial(_stackmask, lane_head=lane_head)
    rh = jax.lax.broadcasted_iota(jnp.int32, (l, l), 0) // RWKV_HEAD
    ch = jax.lax.broadcasted_iota(jnp.int32, (l, l), 1) // RWKV_HEAD
    for w in units:
        w["from_s"] = _dot_nt(w["lhs"], w["s"].astype(BF16))
    for w in units:
        u = w["from_s"][:c] + w["from_v"][:c]
        w["sa"] = _dot(w["p"], bd(u.astype(BF16)))
    outs = []
    for w in units:
        y = None
        if with_y:
            y = w["from_s"][c:] + w["from_v"][c:] + _dot(w["a_rb"], bd(w["sa"].astype(BF16)))
        sav_t = jnp.concatenate([w["sa"], w["v"]], axis=0).T.astype(BF16)
        upd = _dot(sav_t, w["bk"])
        outs.append((w["s"] * w["dtot"] + jnp.where(rh == ch, upd, 0.0), y))
    return outs


def _scan_kernel(*refs, with_y):
    ins = refs[:13]
    s0_ref = ins[12]
    if with_y:
        yf_ref, yb_ref, sfin_ref, s_sc = refs[13:]
        y_refs = (yf_ref, yb_ref)
    else:
        sfin_ref, s_sc = refs[13:]
        y_refs = (None, None)
    ci = pl.program_id(1)

    @pl.when(ci == 0)
    def _():
        s_sc[...] = s0_ref[...]

    c = SCAN_CHUNK
    n_sub = ins[0].shape[1] // c
    steps = [[] for _ in range(n_sub)]
    for n in range(s_sc.shape[0]):
        for d in range(2):
            r_ref, v_ref, kk_ref, cum_ref, kd_ref, ic_ref = ins[6 * d:6 * d + 6]
            reverse = d == 1
            for k in range(n_sub):
                rows = pl.ds((n_sub - 1 - k if reverse else k) * c, c)

                def load(n=n, reverse=reverse, r_ref=r_ref, v_ref=v_ref, kk_ref=kk_ref, cum_ref=cum_ref,
                         kd_ref=kd_ref, ic_ref=ic_ref, rows=rows, sl=None):
                    cum = cum_ref[n, rows, sl]
                    row = jax.lax.broadcasted_iota(jnp.int32, cum.shape, 0)
                    if reverse:
                        cum_excl = jnp.where(row == c - 1, 0.0, pltpu.roll(cum, c - 1, axis=0))
                        tot = cum[0:1, :]
                    else:
                        cum_excl = jnp.where(row == 0, 0.0, pltpu.roll(cum, 1, axis=0))
                        tot = cum[c - 1:c, :]
                    kk = kk_ref[n, rows, sl].astype(F32)
                    return dict(r=r_ref[n, rows, sl].astype(F32), v=v_ref[n, rows, sl].astype(F32), kk=kk,
                                kd=kd_ref[n, rows, sl].astype(F32), b=kk * ic_ref[n, rows, sl].astype(F32),
                                dinc=jnp.exp(cum), dexc=jnp.exp(cum_excl), dinv=jnp.exp(-cum),
                                dend=jnp.exp(tot - cum), dtot=jnp.exp(tot))

                for g in range(r_ref.shape[-1] // GROUP_LANES):
                    sl = slice(g * GROUP_LANES, (g + 1) * GROUP_LANES)
                    steps[k].append(dict(load=functools.partial(load, sl=sl), reverse=reverse, n=n, d=d, g=g,
                                         sl=sl, rows=rows))

    _scan_prepare([w for units in steps for w in units], with_y)
    state = {}
    for units in steps:
        for w in units:
            key = (w["n"], w["d"], w["g"])
            w["s"] = state[key] if key in state else s_sc[key]
        for w, (s_new, y) in zip(units, _scan_apply(units, with_y)):
            state[(w["n"], w["d"], w["g"])] = s_new
            if with_y:
                y_refs[w["d"]][w["n"], w["rows"], w["sl"]] = y
    for key, s_new in state.items():
        s_sc[key] = s_new

    @pl.when(ci == pl.num_programs(1) - 1)
    def _():
        sfin_ref[...] = s_sc[...]


def _scan(q, s0, with_y):
    bsz, t, rw = q["r"].shape
    c = SCAN_CHUNK * SCAN_SUB
    assert t % c == 0
    nb = SCAN_BATCH if bsz % SCAN_BATCH == 0 else 1
    nc = t // c
    ng = rw // GROUP_LANES
    fwd = pl.BlockSpec((nb, c, rw), lambda b, i: (b, i, 0))
    bwd = pl.BlockSpec((nb, c, rw), lambda b, i: (b, nc - 1 - i, 0))
    s_spec = pl.BlockSpec((nb, 2, ng, GROUP_LANES, GROUP_LANES), lambda b, i: (b, 0, 0, 0, 0))
    args = [q["r"], q["v"], q["kk"], q["cum0"], q["kd0"], q["ic0"],
            q["r"], q["v"], q["kk"], q["cum1"], q["kd1"], q["ic1"], s0]
    in_specs = [fwd] * 6 + [bwd] * 6 + [s_spec]
    y_shape = jax.ShapeDtypeStruct((bsz, t, rw), F32)
    out_specs = ([fwd, bwd] if with_y else []) + [s_spec]
    out_shape = ([y_shape, y_shape] if with_y else []) + [jax.ShapeDtypeStruct(s0.shape, F32)]
    outs = pl.pallas_call(
        functools.partial(_scan_kernel, with_y=with_y),
        grid=(bsz // nb, nc),
        in_specs=in_specs,
        out_specs=out_specs,
        out_shape=out_shape,
        scratch_shapes=[pltpu.VMEM((nb, 2, ng, GROUP_LANES, GROUP_LANES), F32)],
        compiler_params=_cparams(("parallel", "arbitrary")),
        name="scan_y" if with_y else "scan_state",
    )(*args)
    return outs


def _mixout_kernel(yf_ref, yb_ref, r_ref, v_ref, g_ref, kd0_ref, kd1_ref, gm_ref, x_ref, mod_ref,
                   rk_ref, gng_ref, gnb_ref, glg_ref, glb_ref, ws_ref, bs_ref, wout_ref, lng_ref, lnb_ref,
                   bd_ref, o_ref, *, slot):
    bd = bd_ref[...]
    rw = yf_ref.shape[-1]
    inv_n = 1.0 / RWKV_HEAD
    inv_g = 1.0 / GMLP_GROUP
    f32 = lambda ref: ref[...].astype(F32)
    y = yf_ref[...] + yb_ref[...]
    vv = _gelu_tanh(gm_ref[:, rw:].astype(F32))
    mu = _head_sum(y, bd) * inv_n
    mu_v = _head_sum(vv, bd) * inv_g
    bonus_in = f32(r_ref) * (f32(kd0_ref) + f32(kd1_ref)) * rk_ref[...]
    dy = y - mu
    var = _head_sum(dy * dy, bd) * inv_n
    dv = vv - mu_v
    var_v = _head_sum(dv * dv, bd) * inv_g
    bonus = _head_sum(bonus_in, bd) * f32(v_ref)
    yn = dy * jax.lax.rsqrt(var + GN_EPS) * gng_ref[...] + gnb_ref[...]
    out_r = ((yn + bonus) * f32(g_ref)).astype(BF16)
    vn = (dv * jax.lax.rsqrt(var_v + LN_EPS) * glg_ref[...] + glb_ref[...]).astype(BF16)
    o_r = _dot(out_r, wout_ref[0:rw, :])
    u = _gelu_tanh(gm_ref[:, :rw].astype(F32))
    tm = vn.shape[0]
    lane = jax.lax.broadcasted_iota(jnp.int32, (CHUNK, 2 * GMLP_GROUP), 1)
    chunks = []
    for n in range(tm // CHUNK):
        pairs = []
        for gp in range(rw // (2 * GMLP_GROUP)):
            v2 = vn[n * CHUNK:(n + 1) * CHUNK, gp * 2 * GMLP_GROUP:(gp + 1) * 2 * GMLP_GROUP]
            m0 = _dot(ws_ref[2 * gp], v2)
            m1 = _dot(ws_ref[2 * gp + 1], v2)
            pairs.append(jnp.where(lane < GMLP_GROUP, m0, m1))
        chunks.append(jnp.concatenate(pairs, axis=1) + bs_ref[...])
    mixed = jnp.concatenate(chunks, axis=0)
    out_g = (u * mixed).astype(BF16)

    o = o_r + _dot(out_g, wout_ref[rw:, :])
    gate = mod_ref[0, 3 * slot + 2:3 * slot + 3, :]
    z = ALPHA * x_ref[...] + gate * o
    o_ref[...] = _layer_norm(z, lng_ref[...], lnb_ref[...])


def _mixout(yf, yb, q, p_gm, x2, mods, mod_row, params, bd, slot, tm):
    ntok, d = x2.shape
    rw = yf.shape[-1]
    tok = lambda w: pl.BlockSpec((tm, w), lambda i: (i, 0))
    const = lambda a: pl.BlockSpec(a.shape, lambda i: (0,) * a.ndim, pipeline_mode=pl.Buffered(1))
    plist = [params[k] for k in ("r_k", "gn_g", "gn_b", "gm_ln_g", "gm_ln_b", "gm_ws", "gm_bs_full", "w_out",
                                 "ln_g", "ln_b")] + [bd]
    return pl.pallas_call(
        functools.partial(_mixout_kernel, slot=slot),
        grid=(ntok // tm,),
        in_specs=[tok(rw)] * 7 + [tok(p_gm.shape[1]), tok(d),
                                  pl.BlockSpec((1, N_MOD, d), lambda i: (mod_row(i), 0, 0))]
                 + [const(a) for a in plist],
        out_specs=tok(d),
        out_shape=jax.ShapeDtypeStruct((ntok, d), F32),
        compiler_params=_cparams(("parallel",)),
        name="mixout",
    )(yf, yb, q["r"], q["v"], q["g"], q["kd0"], q["kd1"], p_gm, x2, mods, *plist)


def kernel(x, c, ctx, c_ctx, w_ada, b_ada, ln_g, ln_b, ffn_a_wi, ffn_a_wo, ffn_b_wi, ffn_b_wo, w_in, mu_shift,
           w0, w_up, a0, a_up, g_up, k_k, k_a, r_k, gn_g, gn_b, gm_ln_g, gm_ln_b, gm_ws, gm_bs, w_out):
    bsz, seq, d = x.shape
    ctx_len = ctx.shape[1]
    assert w_ada.shape[0] == DEPTH
    rw = w0.shape[-1]
    rwkv_in = mu_shift.shape[-1]
    w_lora, a_lora = w_up.shape[2], a_up.shape[2]
    assert w_lora + a_lora == 128 and g_up.shape[1] == 128 and rwkv_in == 3 * rw + 256
    assert seq % GRID_W == 0 and seq % SCAN_CHUNK == 0 and ctx_len % SCAN_CHUNK == 0 and seq % CHUNK == 0
    i = 0

    n_rows = -(-(bsz + 1) // 8) * 8
    cc = jnp.zeros((n_rows, d), F32).at[:bsz].set(c).at[bsz].set(c_ctx)
    mods = _ada(cc, w_ada[i], b_ada[i]).reshape(n_rows, N_MOD, d)

    x2 = x.reshape(bsz * seq, d)
    c2 = ctx.reshape(bsz * ctx_len, d)
    tm_x = _pick(seq, 512)
    tm_c = _pick(ctx_len, 512)
    row_x = lambda tm: (lambda t: t // (seq // tm))
    row_c = lambda t: bsz

    bf = lambda a: a.astype(BF16)
    wi_a, wo_a = _ffn_weights(ffn_a_wi[i], ffn_a_wo[i])
    x1 = _ffn(x2, mods, row_x(tm_x), wi_a, wo_a, ln_g[i, 0], ln_b[i, 0], 0, tm_x)
    c1 = _ffn(c2, mods, row_c, wi_a, wo_a, ln_g[i, 0], ln_b[i, 0], 0, tm_c)

    w_rw, w_gm = bf(w_in[i, :, :rwkv_in]), bf(w_in[i, :, rwkv_in:])
    p_rw_x = _inproj(x1, mods, row_x(tm_x), w_rw, 1, tm_x, rwkv_in // 2)
    p_rw_c = _inproj(c1, mods, row_c, w_rw, 1, tm_c, rwkv_in // 2)

    zpad = lambda a, lo, hi: jnp.pad(a, ((0, 0), (lo, hi), (0, 0)))
    lanes = jnp.arange(GROUP_LANES) // RWKV_HEAD
    bd = (lanes[:, None] == lanes[None, :]).astype(BF16)
    prm = dict(mu=mu_shift[i].reshape(1, rwkv_in), w0=w0[i], w_up=bf(zpad(w_up[i], 0, a_lora)), a0=a0[i],
               a_up=bf(zpad(a_up[i], w_lora, 0)), g_up=bf(g_up[i]), k_k=k_k[i].reshape(1, rw),
               k_a=k_a[i].reshape(1, rw))
    tm_p = _pick(seq, 256)
    q_x = _prep(p_rw_x, prm, bd, seq, tm_p, gm=(x1, mods, row_x(tm_p), w_gm, 1))
    q_c = _prep(p_rw_c, prm, bd, ctx_len, ctx_len)
    p_gm_x = q_x.pop("p_gm")
    q_x3 = {k: a.reshape(bsz, seq, rw) for k, a in q_x.items()}
    q_c3 = {k: a.reshape(bsz, ctx_len, rw) for k, a in q_c.items()}

    ng = rw // GROUP_LANES
    s_zero = jnp.zeros((bsz, 2, ng, GROUP_LANES, GROUP_LANES), F32)
    (s_ctx,) = _scan(q_c3, s_zero, with_y=False)
    yf, yb, _ = _scan(q_x3, s_ctx, with_y=True)

    out_prm = dict(r_k=r_k[i].reshape(1, rw), gn_g=gn_g[i].reshape(1, rw), gn_b=gn_b[i].reshape(1, rw),
                   gm_ln_g=gm_ln_g[i].reshape(1, rw), gm_ln_b=gm_ln_b[i].reshape(1, rw), gm_ws=bf(gm_ws[i]),
                   gm_bs_full=jnp.repeat(gm_bs[i].T, GMLP_GROUP, axis=1), w_out=bf(w_out[i]),
                   ln_g=ln_g[i, 1].reshape(1, d), ln_b=ln_b[i, 1].reshape(1, d))
    tm_o = _pick(seq, 256)
    x2b = _mixout(yf.reshape(bsz * seq, rw), yb.reshape(bsz * seq, rw), q_x, p_gm_x, x1, mods, row_x(tm_o),
                  out_prm, bd, 1, tm_o)

    wi_b, wo_b = _ffn_weights(ffn_b_wi[i], ffn_b_wo[i])
    x3 = _ffn(x2b, mods, row_x(tm_x), wi_b, wo_b, ln_g[i, 2], ln_b[i, 2], 2, tm_x)
    return x3.reshape(bsz, seq, d)
```

```python
import functools
import math

import jax
import jax.numpy as jnp
from jax.experimental import pallas as pl
from jax.experimental.pallas import tpu as pltpu

F32 = jnp.float32
BF16 = jnp.bfloat16

GRID_W = 64
RWKV_HEAD = 64
CHUNK = 128
GMLP_GROUP = 64
N_MOD = 9
LN_EPS = 1e-5
GN_EPS = 64e-5
DEPTH = 1
ALPHA = (2.0 * DEPTH) ** 0.25

SCAN_CHUNK = 64
HEADS_PER_GROUP = 4
GROUP_LANES = HEADS_PER_GROUP * RWKV_HEAD
SCAN_BATCH = 2
SCAN_SUB = 2
VMEM_LIMIT = 56 * 1024 * 1024


def _cparams(sem):
    return pltpu.CompilerParams(dimension_semantics=sem, vmem_limit_bytes=VMEM_LIMIT)


def _pick(n, pref):
    t = min(n, pref)
    while n % t:
        t -= 64
    return t


def _sigmoid(z):
    return 1.0 / (1.0 + jnp.exp(-z))


def _silu(z):
    return z * _sigmoid(z)


def _gelu_tanh(z):
    return 0.5 * z * (1.0 + jnp.tanh(math.sqrt(2.0 / math.pi) * (z + 0.044715 * (z * z * z))))


def _layer_norm(z, g, b):
    mu = jnp.mean(z, axis=-1, keepdims=True)
    d = z - mu
    var = jnp.mean(d * d, axis=-1, keepdims=True)
    return d * jax.lax.rsqrt(var + LN_EPS) * g + b


def _dot(a, b):
    return jnp.dot(a, b, preferred_element_type=F32)


def _dot_nt(a, b):
    return jax.lax.dot_general(a, b, (((1,), (1,)), ((), ())), preferred_element_type=F32)


def _group_sum(z, bd):
    return _dot(z.astype(BF16), bd)


def _head_sum(z, bd):
    w = z.shape[-1]
    return jnp.concatenate([_group_sum(z[:, s:s + GROUP_LANES], bd) for s in range(0, w, GROUP_LANES)], axis=1)


def _ada_kernel(c_ref, w_ref, b_ref, o_ref):
    a = _silu(c_ref[...]).astype(BF16)
    o_ref[...] = _dot(a, w_ref[...].astype(BF16)) + b_ref[...]


def _ada(cc, w_ada, b_ada):
    m, d = cc.shape
    n = w_ada.shape[1]
    tn = 1024
    return pl.pallas_call(
        _ada_kernel,
        grid=(n // tn,),
        in_specs=[pl.BlockSpec((m, d), lambda j: (0, 0)),
                  pl.BlockSpec((d, tn), lambda j: (0, j)),
                  pl.BlockSpec((1, tn), lambda j: (0, j))],
        out_specs=pl.BlockSpec((m, tn), lambda j: (0, j)),
        out_shape=jax.ShapeDtypeStruct((m, n), F32),
        compiler_params=_cparams(("arbitrary",)),
        name="ada",
    )(cc, w_ada, b_ada.reshape(1, n))


def _ffn_kernel(x_ref, mod_ref, wig_ref, wiu_ref, wo_ref, g_ref, b_ref, o_ref, h_sc, acc_sc, *, slot):
    f = pl.program_id(1)

    @pl.when(f == 0)
    def _():
        shift = mod_ref[0, 3 * slot:3 * slot + 1, :]
        scale = mod_ref[0, 3 * slot + 1:3 * slot + 2, :]
        h_sc[...] = (x_ref[...] * (1.0 + scale) + shift).astype(BF16)
        acc_sc[...] = jnp.zeros_like(acc_sc)

    h = h_sc[...]
    half = wo_ref.shape[0] // 2
    cols = [slice(j * half, (j + 1) * half) for j in range(2)]
    gate_up = [(_dot(h, wig_ref[0, :, c]), _dot(h, wiu_ref[0, :, c])) for c in cols]
    out = None
    for c, (gate, up) in zip(cols, gate_up):
        part = _dot((_silu(gate) * up).astype(BF16), wo_ref[c, :])
        out = part if out is None else out + part
    acc_sc[...] += out

    @pl.when(f == pl.num_programs(1) - 1)
    def _():
        gmod = mod_ref[0, 3 * slot + 2:3 * slot + 3, :]
        z = ALPHA * x_ref[...] + 0.5 * gmod * acc_sc[...]
        o_ref[...] = _layer_norm(z, g_ref[...], b_ref[...])


FFN_TF = 512


def _cast_tiles_kernel(w_ref, o_ref):
    o_ref[0] = w_ref[...].astype(BF16)


def _cast_col_tiles(w, tn):
    d, n = w.shape
    return pl.pallas_call(
        _cast_tiles_kernel,
        grid=(n // tn,),
        in_specs=[pl.BlockSpec((d, tn), lambda j: (0, j))],
        out_specs=pl.BlockSpec((1, d, tn), lambda j: (j, 0, 0)),
        out_shape=jax.ShapeDtypeStruct((n // tn, d, tn), BF16),
        compiler_params=_cparams(("parallel",)),
        name="cast_tiles",
    )(w)


def _ffn_weights(wi, wo):
    return _cast_col_tiles(wi, FFN_TF), wo.astype(BF16)


def _ffn(x2, mods, mod_row, wi, wo, ln_g, ln_b, slot, tm):
    ntok, d = x2.shape
    dff = wo.shape[0]
    tf = FFN_TF
    nf = dff // tf
    return pl.pallas_call(
        functools.partial(_ffn_kernel, slot=slot),
        grid=(ntok // tm, nf),
        in_specs=[pl.BlockSpec((tm, d), lambda i, f: (i, 0)),
                  pl.BlockSpec((1, N_MOD, d), lambda i, f: (mod_row(i), 0, 0)),
                  pl.BlockSpec((1, d, tf), lambda i, f: (f, 0, 0)),
                  pl.BlockSpec((1, d, tf), lambda i, f: (nf + f, 0, 0)),
                  pl.BlockSpec((tf, d), lambda i, f: (f, 0)),
                  pl.BlockSpec((1, d), lambda i, f: (0, 0)),
                  pl.BlockSpec((1, d), lambda i, f: (0, 0))],
        out_specs=pl.BlockSpec((tm, d), lambda i, f: (i, 0)),
        out_shape=jax.ShapeDtypeStruct((ntok, d), F32),
        scratch_shapes=[pltpu.VMEM((tm, d), BF16), pltpu.VMEM((tm, d), F32)],
        compiler_params=_cparams(("parallel", "arbitrary")),
        name=f"ffn{slot}",
    )(x2, mods, wi, wi, wo, ln_g.reshape(1, d), ln_b.reshape(1, d))


def _inproj_kernel(x_ref, mod_ref, w_ref, o_ref, *, slot):
    shift = mod_ref[0, 3 * slot:3 * slot + 1, :]
    scale = mod_ref[0, 3 * slot + 1:3 * slot + 2, :]
    h = (x_ref[...] * (1.0 + scale) + shift).astype(BF16)
    o_ref[...] = _dot(h, w_ref[...])


def _inproj(x2, mods, mod_row, w, slot, tm, tn):
    ntok, d = x2.shape
    n = w.shape[1]
    return pl.pallas_call(
        functools.partial(_inproj_kernel, slot=slot),
        grid=(n // tn, ntok // tm),
        in_specs=[pl.BlockSpec((tm, d), lambda j, i: (i, 0)),
                  pl.BlockSpec((1, N_MOD, d), lambda j, i: (mod_row(i), 0, 0)),
                  pl.BlockSpec((d, tn), lambda j, i: (0, j))],
        out_specs=pl.BlockSpec((tm, tn), lambda j, i: (i, j)),
        out_shape=jax.ShapeDtypeStruct((ntok, n), F32),
        compiler_params=_cparams(("parallel", "parallel")),
        name="inproj",
    )(x2, mods, w)


_PREP_NAMES = ("r", "v", "g", "kk", "kd0", "kd1", "ic0", "ic1", "cum0", "cum1")
_PREP_DTYPES = (BF16,) * 8 + (F32,) * 2


def _prep_kernel(*refs, grid_mode, tiles_per_seq, rw, slot):
    if grid_mode:
        p_ref, up_ref, dn_ref, x_ref, mod_ref, wgm_ref = refs[:6]
        refs = refs[6:]
    else:
        p_ref = refs[0]
        refs = refs[1:]
    (mu_ref, w0_ref, wup_ref, a0_ref, aup_ref, gup_ref, kk_ref, ka_ref, bd_ref, tri0_ref, tri1_ref,
     r_o, v_o, g_o, kk_o, kd0_o, kd1_o, ic0_o, ic1_o, cum0_o, cum1_o) = refs[:21]
    tm = p_ref.shape[0]
    n_gm = 4

    def gm_chunk(j):
        if grid_mode:
            pgm_o = refs[21]
            wn = pgm_o.shape[1] // n_gm
            pgm_o[:, j * wn:(j + 1) * wn] = _dot(h_gm, wgm_ref[:, j * wn:(j + 1) * wn]).astype(BF16)

    def mixed(lo, hi):
        p = p_ref[:, lo:hi]
        width = hi - lo
        row = jax.lax.broadcasted_iota(jnp.int32, (tm, width), 0)
        lane = jax.lax.broadcasted_iota(jnp.int32, (tm, width), 1)
        prev = pltpu.roll(p, 1, axis=0)
        nxt = pltpu.roll(p, tm - 1, axis=0)
        if grid_mode:
            i = pl.program_id(0) % tiles_per_seq
            col = row % GRID_W
            left = jnp.where(col > 0, prev, 0.0)
            right = jnp.where(col < GRID_W - 1, nxt, 0.0)
            up_halo = jnp.where(i > 0, up_ref[:, lo:hi], 0.0)
            dn_halo = jnp.where(i < tiles_per_seq - 1, dn_ref[:, lo:hi], 0.0)
            up = jnp.concatenate([up_halo, p[:tm - GRID_W]], axis=0)
            down = jnp.concatenate([p[GRID_W:], dn_halo], axis=0)
            c4 = lane % 4
            shifted = jnp.where(c4 == 0, left, jnp.where(c4 == 1, right, jnp.where(c4 == 2, up, down)))
        else:
            prev = jnp.where(row > 0, prev, 0.0)
            nxt = jnp.where(row < tm - 1, nxt, 0.0)
            shifted = jnp.where(lane % 2 == 0, prev, nxt)
        return p + (shifted - p) * mu_ref[:, lo:hi]

    if grid_mode:
        shift = mod_ref[0, 3 * slot:3 * slot + 1, :]
        scale = mod_ref[0, 3 * slot + 1:3 * slot + 2, :]
        h_gm = (x_ref[...] * (1.0 + scale) + shift).astype(BF16)
    wa_lo = mixed(3 * rw, 3 * rw + 128)
    g_lo = mixed(3 * rw + 128, 3 * rw + 256)
    tanh_wa = jnp.tanh(wa_lo).astype(BF16)
    wa_bf = wa_lo.astype(BF16)
    sig_g = _sigmoid(g_lo).astype(BF16)
    gm_chunk(0)
    k = mixed(rw, 2 * rw)
    kk = k * kk_ref[...]
    kk_sq = kk * kk
    g_lin = _dot(sig_g, gup_ref[...])
    z_w = [_dot(tanh_wa, wup_ref[d]) for d in range(2)]
    z_a = [_dot(wa_bf, aup_ref[d]) for d in range(2)]
    gm_chunk(1)
    ss = _head_sum(kk_sq, bd_ref[...])
    gm_chunk(2)
    g_o[...] = g_lin.astype(BF16)
    r_o[...] = mixed(0, rw).astype(BF16)
    v_o[...] = mixed(2 * rw, 3 * rw).astype(BF16)
    kk_o[...] = (kk / jnp.maximum(jnp.sqrt(ss), 1e-12)).astype(BF16)
    for d, (tri_ref, cum_o, kd_o, ic_o) in enumerate(((tri0_ref, cum0_o, kd0_o, ic0_o),
                                                      (tri1_ref, cum1_o, kd1_o, ic1_o))):
        lw = -math.exp(-0.5) * _sigmoid(w0_ref[d:d + 1, :] + z_w[d])
        hi = lw.astype(BF16)
        rem = lw - hi.astype(F32)
        mid = rem.astype(BF16)
        lo = (rem - mid.astype(F32)).astype(BF16)
        tri = tri_ref[...]
        cum_o[...] = _dot(tri, hi) + _dot(tri, mid) + _dot(tri, lo)
        iclr = _sigmoid(a0_ref[d:d + 1, :] + z_a[d])
        ic_o[...] = iclr.astype(BF16)
        kd_o[...] = (k * (1.0 + (iclr - 1.0) * ka_ref[...])).astype(BF16)
    gm_chunk(3)


def _prep(p_rw, params, bd, tokens_per_seq, tm, gm=None):
    ntok, width = p_rw.shape
    rw = params["w0"].shape[1]
    grid_mode = gm is not None
    tiles_per_seq = tokens_per_seq // tm
    hb = tm // GRID_W
    nhalo = ntok // GRID_W
    const = lambda a: pl.BlockSpec(a.shape, lambda i: (0,) * a.ndim, pipeline_mode=pl.Buffered(1))
    tok = lambda w: pl.BlockSpec((tm, w), lambda i: (i, 0))
    in_specs = [tok(width)]
    args = [p_rw]
    names, dtypes, widths = list(_PREP_NAMES), list(_PREP_DTYPES), [rw] * 10
    slot = 0
    if grid_mode:
        x2, mods, mod_row, w_gm, slot = gm
        d = x2.shape[1]
        in_specs += [pl.BlockSpec((GRID_W, width), lambda i: (jnp.maximum(i * hb - 1, 0), 0)),
                     pl.BlockSpec((GRID_W, width), lambda i: (jnp.minimum((i + 1) * hb, nhalo - 1), 0)),
                     tok(d), pl.BlockSpec((1, N_MOD, d), lambda i: (mod_row(i), 0, 0)), const(w_gm)]
        args += [p_rw, p_rw, x2, mods, w_gm]
        names.append("p_gm")
        dtypes.append(BF16)
        widths.append(w_gm.shape[1])
    ti = jnp.arange(tm)[:, None]
    tj = jnp.arange(tm)[None, :]
    same = (ti // SCAN_CHUNK) == (tj // SCAN_CHUNK)
    tri = [(same & (tj <= ti)).astype(BF16), (same & (tj >= ti)).astype(BF16)]
    plist = [params[k] for k in ("mu", "w0", "w_up", "a0", "a_up", "g_up", "k_k", "k_a")] + [bd] + tri
    in_specs += [const(a) for a in plist]
    outs = pl.pallas_call(
        functools.partial(_prep_kernel, grid_mode=grid_mode, tiles_per_seq=tiles_per_seq, rw=rw, slot=slot),
        grid=(ntok // tm,),
        in_specs=in_specs,
        out_specs=[tok(w) for w in widths],
        out_shape=[jax.ShapeDtypeStruct((ntok, w), dt) for w, dt in zip(widths, dtypes)],
        compiler_params=_cparams(("parallel",)),
        name="prep_grid" if grid_mode else "prep_seq",
    )(*args, *plist)
    return dict(zip(names, outs))


def _stackmask(z, lane_head):
    return jnp.concatenate([jnp.where(lane_head == h, z, jnp.zeros_like(z)) for h in range(HEADS_PER_GROUP)],
                           axis=0)


def _scan_prepare(units, with_y):
    c, l = SCAN_CHUNK, GROUP_LANES
    gc = HEADS_PER_GROUP * c
    n_lvl = int(math.log2(c))
    t = jax.lax.broadcasted_iota(jnp.int32, (c, gc), 0)
    tj = jax.lax.broadcasted_iota(jnp.int32, (c, gc), 1) % c
    lane_head = jax.lax.broadcasted_iota(jnp.int32, (c, l), 1) // RWKV_HEAD
    bd = functools.partial(_stackmask, lane_head=lane_head)
    eye = jnp.where(tj == t, 1.0, 0.0)

    for w in units:
        w.update(w["load"]())
        at = (-w["kk"] * w["dexc"]).astype(BF16)
        rt = (w["r"] * w["dinc"]).astype(BF16)
        bt = (w["b"] * w["dinv"]).astype(BF16)
        kt = (w["kd"] * w["dinv"]).astype(BF16)
        w["lhs"] = jnp.concatenate([at, rt], axis=0) if with_y else at
        w["a_all"] = _dot_nt(w["lhs"], jnp.concatenate([bd(bt), bd(kt)], axis=0))
        w["bk"] = jnp.concatenate([w["b"] * w["dend"], w["kd"] * w["dend"]], axis=0).astype(BF16)
    for w in units:
        strict = (tj > t) if w["reverse"] else (tj < t)
        a_ab = jnp.where(strict, w["a_all"][:c, :gc], 0.0)
        a_v = jnp.where(strict, w["a_all"][:c, gc:], 0.0)
        if with_y:
            incl = (tj >= t) if w["reverse"] else (tj <= t)
            a_v = jnp.concatenate([a_v, jnp.where(incl, w["a_all"][c:, gc:], 0.0)], axis=0)
            w["a_rb"] = jnp.where(incl, w["a_all"][c:, :gc], 0.0).astype(BF16)
        w["from_v"] = _dot(a_v.astype(BF16), bd(w["v"].astype(BF16)))
        w["p"] = eye + a_ab
        w["apow"] = a_ab.astype(BF16)
    for w in units:
        w["apow"] = _dot(w["apow"], bd(w["apow"])).astype(BF16)
    for lvl in range(1, n_lvl):
        last = lvl == n_lvl - 1
        for w in units:
            p_bf = w["p"].astype(BF16)
            if last:
                w["p"] = (w["p"] + _dot(p_bf, bd(w["apow"]))).astype(BF16)
            else:
                both = _dot(jnp.concatenate([p_bf, w["apow"]], axis=0), bd(w["apow"]))
                w["p"] = w["p"] + both[:c]
                w["apow"] = both[c:].astype(BF16)


def _scan_apply(units, with_y):
    c, l = SCAN_CHUNK, GROUP_LANES
    lane_head = jax.lax.broadcasted_iota(jnp.int32, (c, l), 1) // RWKV_HEAD
    bd = functools.partial(_stackmask, lane_head=lane_head)
    rh = jax.lax.broadcasted_iota(jnp.int32, (l, l), 0) // RWKV_HEAD
    ch = jax.lax.broadcasted_iota(jnp.int32, (l, l), 1) // RWKV_HEAD
    for w in units:
        w["from_s"] = _dot_nt(w["lhs"], w["s"].astype(BF16))
    for w in units:
        u = w["from_s"][:c] + w["from_v"][:c]
        w["sa"] = _dot(w["p"], bd(u.astype(BF16)))
    outs = []
    for w in units:
        y = None
        if with_y:
            y = w["from_s"][c:] + w["from_v"][c:] + _dot(w["a_rb"], bd(w["sa"].astype(BF16)))
        sav_t = jnp.concatenate([w["sa"], w["v"]], axis=0).T.astype(BF16)
        upd = _dot(sav_t, w["bk"])
        outs.append((w["s"] * w["dtot"] + jnp.where(rh == ch, upd, 0.0), y))
    return outs


def _scan_kernel(*refs, with_y):
    ins = refs[:13]
    s0_ref = ins[12]
    if with_y:
        yf_ref, yb_ref, sfin_ref, s_sc = refs[13:]
        y_refs = (yf_ref, yb_ref)
    else:
        sfin_ref, s_sc = refs[13:]
        y_refs = (None, None)
    ci = pl.program_id(1)

    @pl.when(ci == 0)
    def _():
        s_sc[...] = s0_ref[...]

    c = SCAN_CHUNK
    n_sub = ins[0].shape[1] // c
    steps = [[] for _ in range(n_sub)]
    for n in range(s_sc.shape[0]):
        for d in range(2):
            r_ref, v_ref, kk_ref, cum_ref, kd_ref, ic_ref = ins[6 * d:6 * d + 6]
            reverse = d == 1
            for k in range(n_sub):
                rows = pl.ds((n_sub - 1 - k if reverse else k) * c, c)

                def load(n=n, reverse=reverse, r_ref=r_ref, v_ref=v_ref, kk_ref=kk_ref, cum_ref=cum_ref,
                         kd_ref=kd_ref, ic_ref=ic_ref, rows=rows, sl=None):
                    cum = cum_ref[n, rows, sl]
                    row = jax.lax.broadcasted_iota(jnp.int32, cum.shape, 0)
                    if reverse:
                        cum_excl = jnp.where(row == c - 1, 0.0, pltpu.roll(cum, c - 1, axis=0))
                        tot = cum[0:1, :]
                    else:
                        cum_excl = jnp.where(row == 0, 0.0, pltpu.roll(cum, 1, axis=0))
                        tot = cum[c - 1:c, :]
                    kk = kk_ref[n, rows, sl].astype(F32)
                    return dict(r=r_ref[n, rows, sl].astype(F32), v=v_ref[n, rows, sl].astype(F32), kk=kk,
                                kd=kd_ref[n, rows, sl].astype(F32), b=kk * ic_ref[n, rows, sl].astype(F32),
                                dinc=jnp.exp(cum), dexc=jnp.exp(cum_excl), dinv=jnp.exp(-cum),
                                dend=jnp.exp(tot - cum), dtot=jnp.exp(tot))

                for g in range(r_ref.shape[-1] // GROUP_LANES):
                    sl = slice(g * GROUP_LANES, (g + 1) * GROUP_LANES)
                    steps[k].append(dict(load=functools.partial(load, sl=sl), reverse=reverse, n=n, d=d, g=g,
                                         sl=sl, rows=rows))

    _scan_prepare([w for units in steps for w in units], with_y)
    state = {}
    for units in steps:
        for w in units:
            key = (w["n"], w["d"], w["g"])
            w["s"] = state[key] if key in state else s_sc[key]
        for w, (s_new, y) in zip(units, _scan_apply(units, with_y)):
            state[(w["n"], w["d"], w["g"])] = s_new
            if with_y:
                y_refs[w["d"]][w["n"], w["rows"], w["sl"]] = y
    for key, s_new in state.items():
        s_sc[key] = s_new

    @pl.when(ci == pl.num_programs(1) - 1)
    def _():
        sfin_ref[...] = s_sc[...]


def _scan(q, s0, with_y, n_sub):
    bsz, t, rw = q["r"].shape
    c = SCAN_CHUNK * n_sub
    assert t % c == 0
    nb = SCAN_BATCH if bsz % SCAN_BATCH == 0 else 1
    nc = t // c
    ng = rw // GROUP_LANES
    fwd = pl.BlockSpec((nb, c, rw), lambda b, i: (b, i, 0))
    bwd = pl.BlockSpec((nb, c, rw), lambda b, i: (b, nc - 1 - i, 0))
    s_spec = pl.BlockSpec((nb, 2, ng, GROUP_LANES, GROUP_LANES), lambda b, i: (b, 0, 0, 0, 0))
    args = [q["r"], q["v"], q["kk"], q["cum0"], q["kd0"], q["ic0"],
            q["r"], q["v"], q["kk"], q["cum1"], q["kd1"], q["ic1"], s0]
    in_specs = [fwd] * 6 + [bwd] * 6 + [s_spec]
    y_shape = jax.ShapeDtypeStruct((bsz, t, rw), F32)
    out_specs = ([fwd, bwd] if with_y else []) + [s_spec]
    out_shape = ([y_shape, y_shape] if with_y else []) + [jax.ShapeDtypeStruct(s0.shape, F32)]
    outs = pl.pallas_call(
        functools.partial(_scan_kernel, with_y=with_y),
        grid=(bsz // nb, nc),
        in_specs=in_specs,
        out_specs=out_specs,
        out_shape=out_shape,
        scratch_shapes=[pltpu.VMEM((nb, 2, ng, GROUP_LANES, GROUP_LANES), F32)],
        compiler_params=_cparams(("parallel", "arbitrary")),
        name="scan_y" if with_y else "scan_state",
    )(*args)
    return outs


def _mixout_kernel(yf_ref, yb_ref, r_ref, v_ref, g_ref, kd0_ref, kd1_ref, gm_ref, x_ref, mod_ref,
                   rk_ref, gng_ref, gnb_ref, glg_ref, glb_ref, ws_ref, bs_ref, wout_ref, lng_ref, lnb_ref,
                   bd_ref, o_ref, *, slot):
    bd = bd_ref[...]
    rw = yf_ref.shape[-1]
    inv_n = 1.0 / RWKV_HEAD
    inv_g = 1.0 / GMLP_GROUP
    f32 = lambda ref: ref[...].astype(F32)
    y = yf_ref[...] + yb_ref[...]
    vv = _gelu_tanh(gm_ref[:, rw:].astype(F32))
    mu = _head_sum(y, bd) * inv_n
    mu_v = _head_sum(vv, bd) * inv_g
    bonus_in = f32(r_ref) * (f32(kd0_ref) + f32(kd1_ref)) * rk_ref[...]
    dy = y - mu
    var = _head_sum(dy * dy, bd) * inv_n
    dv = vv - mu_v
    var_v = _head_sum(dv * dv, bd) * inv_g
    bonus = _head_sum(bonus_in, bd) * f32(v_ref)
    yn = dy * jax.lax.rsqrt(var + GN_EPS) * gng_ref[...] + gnb_ref[...]
    out_r = ((yn + bonus) * f32(g_ref)).astype(BF16)
    vn = (dv * jax.lax.rsqrt(var_v + LN_EPS) * glg_ref[...] + glb_ref[...]).astype(BF16)
    o_r = _dot(out_r, wout_ref[0:rw, :])
    u = _gelu_tanh(gm_ref[:, :rw].astype(F32))
    tm = vn.shape[0]
    lane = jax.lax.broadcasted_iota(jnp.int32, (CHUNK, 2 * GMLP_GROUP), 1)
    chunks = []
    for n in range(tm // CHUNK):
        pairs = []
        for gp in range(rw // (2 * GMLP_GROUP)):
            v2 = vn[n * CHUNK:(n + 1) * CHUNK, gp * 2 * GMLP_GROUP:(gp + 1) * 2 * GMLP_GROUP]
            m0 = _dot(ws_ref[2 * gp], v2)
            m1 = _dot(ws_ref[2 * gp + 1], v2)
            pairs.append(jnp.where(lane < GMLP_GROUP, m0, m1))
        chunks.append(jnp.concatenate(pairs, axis=1) + bs_ref[...])
    mixed = jnp.concatenate(chunks, axis=0)
    out_g = (u * mixed).astype(BF16)

    o = o_r + _dot(out_g, wout_ref[rw:, :])
    gate = mod_ref[0, 3 * slot + 2:3 * slot + 3, :]
    z = ALPHA * x_ref[...] + gate * o
    o_ref[...] = _layer_norm(z, lng_ref[...], lnb_ref[...])


def _mixout(yf, yb, q, p_gm, x2, mods, mod_row, params, bd, slot, tm):
    ntok, d = x2.shape
    rw = yf.shape[-1]
    tok = lambda w: pl.BlockSpec((tm, w), lambda i: (i, 0))
    const = lambda a: pl.BlockSpec(a.shape, lambda i: (0,) * a.ndim, pipeline_mode=pl.Buffered(1))
    plist = [params[k] for k in ("r_k", "gn_g", "gn_b", "gm_ln_g", "gm_ln_b", "gm_ws", "gm_bs_full", "w_out",
                                 "ln_g", "ln_b")] + [bd]
    return pl.pallas_call(
        functools.partial(_mixout_kernel, slot=slot),
        grid=(ntok // tm,),
        in_specs=[tok(rw)] * 7 + [tok(p_gm.shape[1]), tok(d),
                                  pl.BlockSpec((1, N_MOD, d), lambda i: (mod_row(i), 0, 0))]
                 + [const(a) for a in plist],
        out_specs=tok(d),
        out_shape=jax.ShapeDtypeStruct((ntok, d), F32),
        compiler_params=_cparams(("parallel",)),
        name="mixout",
    )(yf, yb, q["r"], q["v"], q["g"], q["kd0"], q["kd1"], p_gm, x2, mods, *plist)


def kernel(x, c, ctx, c_ctx, w_ada, b_ada, ln_g, ln_b, ffn_a_wi, ffn_a_wo, ffn_b_wi, ffn_b_wo, w_in, mu_shift,
           w0, w_up, a0, a_up, g_up, k_k, k_a, r_k, gn_g, gn_b, gm_ln_g, gm_ln_b, gm_ws, gm_bs, w_out):
    bsz, seq, d = x.shape
    ctx_len = ctx.shape[1]
    assert w_ada.shape[0] == DEPTH
    rw = w0.shape[-1]
    rwkv_in = mu_shift.shape[-1]
    w_lora, a_lora = w_up.shape[2], a_up.shape[2]
    assert w_lora + a_lora == 128 and g_up.shape[1] == 128 and rwkv_in == 3 * rw + 256
    assert seq % GRID_W == 0 and seq % SCAN_CHUNK == 0 and ctx_len % SCAN_CHUNK == 0 and seq % CHUNK == 0
    i = 0

    n_rows = -(-(bsz + 1) // 8) * 8
    cc = jnp.zeros((n_rows, d), F32).at[:bsz].set(c).at[bsz].set(c_ctx)
    mods = _ada(cc, w_ada[i], b_ada[i]).reshape(n_rows, N_MOD, d)

    x2 = x.reshape(bsz * seq, d)
    c2 = ctx.reshape(bsz * ctx_len, d)
    tm_x = _pick(seq, 512)
    tm_c = _pick(ctx_len, 512)
    row_x = lambda tm: (lambda t: t // (seq // tm))
    row_c = lambda t: bsz

    bf = lambda a: a.astype(BF16)
    wi_a, wo_a = _ffn_weights(ffn_a_wi[i], ffn_a_wo[i])
    x1 = _ffn(x2, mods, row_x(tm_x), wi_a, wo_a, ln_g[i, 0], ln_b[i, 0], 0, tm_x)
    c1 = _ffn(c2, mods, row_c, wi_a, wo_a, ln_g[i, 0], ln_b[i, 0], 0, tm_c)

    w_rw, w_gm = bf(w_in[i, :, :rwkv_in]), bf(w_in[i, :, rwkv_in:])
    p_rw_x = _inproj(x1, mods, row_x(tm_x), w_rw, 1, tm_x, rwkv_in // 2)
    p_rw_c = _inproj(c1, mods, row_c, w_rw, 1, tm_c, rwkv_in // 2)

    zpad = lambda a, lo, hi: jnp.pad(a, ((0, 0), (lo, hi), (0, 0)))
    lanes = jnp.arange(GROUP_LANES) // RWKV_HEAD
    bd = (lanes[:, None] == lanes[None, :]).astype(BF16)
    prm = dict(mu=mu_shift[i].reshape(1, rwkv_in), w0=w0[i], w_up=bf(zpad(w_up[i], 0, a_lora)), a0=a0[i],
               a_up=bf(zpad(a_up[i], w_lora, 0)), g_up=bf(g_up[i]), k_k=k_k[i].reshape(1, rw),
               k_a=k_a[i].reshape(1, rw))
    tm_p = _pick(seq, 256)
    q_x = _prep(p_rw_x, prm, bd, seq, tm_p, gm=(x1, mods, row_x(tm_p), w_gm, 1))
    q_c = _prep(p_rw_c, prm, bd, ctx_len, ctx_len)
    p_gm_x = q_x.pop("p_gm")
    q_x3 = {k: a.reshape(bsz, seq, rw) for k, a in q_x.items()}
    q_c3 = {k: a.reshape(bsz, ctx_len, rw) for k, a in q_c.items()}

    ng = rw // GROUP_LANES
    s_zero = jnp.zeros((bsz, 2, ng, GROUP_LANES, GROUP_LANES), F32)
    (s_ctx,) = _scan(q_c3, s_zero, with_y=False, n_sub=1)
    yf, yb, _ = _scan(q_x3, s_ctx, with_y=True, n_sub=SCAN_SUB)

    out_prm = dict(r_k=r_k[i].reshape(1, rw), gn_g=gn_g[i].reshape(1, rw), gn_b=gn_b[i].reshape(1, rw),
                   gm_ln_g=gm_ln_g[i].reshape(1, rw), gm_ln_b=gm_ln_b[i].reshape(1, rw), gm_ws=bf(gm_ws[i]),
                   gm_bs_full=jnp.repeat(gm_bs[i].T, GMLP_GROUP, axis=1), w_out=bf(w_out[i]),
                   ln_g=ln_g[i, 1].reshape(1, d), ln_b=ln_b[i, 1].reshape(1, d))
    tm_o = _pick(seq, 256)
    x2b = _mixout(yf.reshape(bsz * seq, rw), yb.reshape(bsz * seq, rw), q_x, p_gm_x, x1, mods, row_x(tm_o),
                  out_prm, bd, 1, tm_o)

    wi_b, wo_b = _ffn_weights(ffn_b_wi[i], ffn_b_wo[i])
    x3 = _ffn(x2b, mods, row_x(tm_x), wi_b, wo_b, ln_g[i, 2], ln_b[i, 2], 2, tm_x)
    return x3.reshape(bsz, seq, d)
```

```python
import functools
import math

import jax
import jax.numpy as jnp
from jax.experimental import pallas as pl
from jax.experimental.pallas import tpu as pltpu

F32 = jnp.float32
BF16 = jnp.bfloat16

GRID_W = 64
RWKV_HEAD = 64
CHUNK = 128
GMLP_GROUP = 64
N_MOD = 9
LN_EPS = 1e-5
GN_EPS = 64e-5
DEPTH = 1
ALPHA = (2.0 * DEPTH) ** 0.25

SCAN_CHUNK = 64
HEADS_PER_GROUP = 4
GROUP_LANES = HEADS_PER_GROUP * RWKV_HEAD
SCAN_BATCH = 2
SCAN_SUB = 2
VMEM_LIMIT = 56 * 1024 * 1024


def _cparams(sem, vmem_limit=VMEM_LIMIT):
    return pltpu.CompilerParams(dimension_semantics=sem, vmem_limit_bytes=vmem_limit)


def _pick(n, pref):
    t = min(n, pref)
    while n % t:
        t -= 64
    return t


def _sigmoid(z):
    return 1.0 / (1.0 + jnp.exp(-z))


def _silu(z):
    return z * _sigmoid(z)


def _gelu_tanh(z):
    return 0.5 * z * (1.0 + jnp.tanh(math.sqrt(2.0 / math.pi) * (z + 0.044715 * (z * z * z))))


def _layer_norm(z, g, b):
    mu = jnp.mean(z, axis=-1, keepdims=True)
    d = z - mu
    var = jnp.mean(d * d, axis=-1, keepdims=True)
    return d * jax.lax.rsqrt(var + LN_EPS) * g + b


def _dot(a, b):
    return jnp.dot(a, b, preferred_element_type=F32)


def _dot_nt(a, b):
    return jax.lax.dot_general(a, b, (((1,), (1,)), ((), ())), preferred_element_type=F32)


def _group_sum(z, bd):
    return _dot(z.astype(BF16), bd)


def _head_sum(z, bd):
    w = z.shape[-1]
    return jnp.concatenate([_group_sum(z[:, s:s + GROUP_LANES], bd) for s in range(0, w, GROUP_LANES)], axis=1)


def _ada_kernel(c_ref, w_ref, b_ref, o_ref):
    a = _silu(c_ref[...]).astype(BF16)
    o_ref[...] = _dot(a, w_ref[...].astype(BF16)) + b_ref[...]


def _ada(cc, w_ada, b_ada):
    m, d = cc.shape
    n = w_ada.shape[1]
    tn = 1024
    return pl.pallas_call(
        _ada_kernel,
        grid=(n // tn,),
        in_specs=[pl.BlockSpec((m, d), lambda j: (0, 0)),
                  pl.BlockSpec((d, tn), lambda j: (0, j)),
                  pl.BlockSpec((1, tn), lambda j: (0, j))],
        out_specs=pl.BlockSpec((m, tn), lambda j: (0, j)),
        out_shape=jax.ShapeDtypeStruct((m, n), F32),
        compiler_params=_cparams(("arbitrary",)),
        name="ada",
    )(cc, w_ada, b_ada.reshape(1, n))


def _ffn_kernel(x_ref, mod_ref, wig_ref, wiu_ref, wo_ref, g_ref, b_ref, o_ref, h_sc, *, slot):
    f = pl.program_id(1)

    @pl.when(f == 0)
    def _():
        shift = mod_ref[0, 3 * slot:3 * slot + 1, :]
        scale = mod_ref[0, 3 * slot + 1:3 * slot + 2, :]
        h_sc[...] = (x_ref[...] * (1.0 + scale) + shift).astype(BF16)
        o_ref[...] = jnp.zeros_like(o_ref)

    half = wo_ref.shape[0] // 2
    cols = [slice(j * half, (j + 1) * half) for j in range(2)]
    n_rows = min(FFN_ROWS, h_sc.shape[0])
    for r0 in range(0, h_sc.shape[0], n_rows):
        rows = slice(r0, r0 + n_rows)
        h = h_sc[rows, :]
        gate_up = [(_dot(h, wig_ref[0, :, c]), _dot(h, wiu_ref[0, :, c])) for c in cols]
        out = None
        for c, (gate, up) in zip(cols, gate_up):
            part = _dot((_silu(gate) * up).astype(BF16), wo_ref[c, :])
            out = part if out is None else out + part
        o_ref[rows, :] += out

    @pl.when(f == pl.num_programs(1) - 1)
    def _():
        gmod = mod_ref[0, 3 * slot + 2:3 * slot + 3, :]
        z = ALPHA * x_ref[...] + 0.5 * gmod * o_ref[...]
        o_ref[...] = _layer_norm(z, g_ref[...], b_ref[...])


FFN_TF = 512
FFN_TM = 1024
FFN_ROWS = 512
FFN_VMEM_LIMIT = 62 * 1024 * 1024


def _cast_tiles_kernel(w_ref, o_ref):
    o_ref[0] = w_ref[...].astype(BF16)


def _cast_col_tiles(w, tn):
    d, n = w.shape
    return pl.pallas_call(
        _cast_tiles_kernel,
        grid=(n // tn,),
        in_specs=[pl.BlockSpec((d, tn), lambda j: (0, j))],
        out_specs=pl.BlockSpec((1, d, tn), lambda j: (j, 0, 0)),
        out_shape=jax.ShapeDtypeStruct((n // tn, d, tn), BF16),
        compiler_params=_cparams(("parallel",)),
        name="cast_tiles",
    )(w)


def _ffn_weights(wi, wo):
    return _cast_col_tiles(wi, FFN_TF), wo.astype(BF16)


def _ffn(x2, mods, mod_row, wi, wo, ln_g, ln_b, slot, tm):
    ntok, d = x2.shape
    dff = wo.shape[0]
    tf = FFN_TF
    nf = dff // tf
    return pl.pallas_call(
        functools.partial(_ffn_kernel, slot=slot),
        grid=(ntok // tm, nf),
        in_specs=[pl.BlockSpec((tm, d), lambda i, f: (i, 0)),
                  pl.BlockSpec((1, N_MOD, d), lambda i, f: (mod_row(i), 0, 0)),
                  pl.BlockSpec((1, d, tf), lambda i, f: (f, 0, 0)),
                  pl.BlockSpec((1, d, tf), lambda i, f: (nf + f, 0, 0)),
                  pl.BlockSpec((tf, d), lambda i, f: (f, 0)),
                  pl.BlockSpec((1, d), lambda i, f: (0, 0)),
                  pl.BlockSpec((1, d), lambda i, f: (0, 0))],
        out_specs=pl.BlockSpec((tm, d), lambda i, f: (i, 0)),
        out_shape=jax.ShapeDtypeStruct((ntok, d), F32),
        scratch_shapes=[pltpu.VMEM((tm, d), BF16)],
        compiler_params=_cparams(("parallel", "arbitrary"), FFN_VMEM_LIMIT),
        name=f"ffn{slot}",
    )(x2, mods, wi, wi, wo, ln_g.reshape(1, d), ln_b.reshape(1, d))


def _inproj_kernel(x_ref, mod_ref, w_ref, o_ref, *, slot):
    shift = mod_ref[0, 3 * slot:3 * slot + 1, :]
    scale = mod_ref[0, 3 * slot + 1:3 * slot + 2, :]
    h = (x_ref[...] * (1.0 + scale) + shift).astype(BF16)
    o_ref[...] = _dot(h, w_ref[...])


def _inproj(x2, mods, mod_row, w, slot, tm, tn):
    ntok, d = x2.shape
    n = w.shape[1]
    return pl.pallas_call(
        functools.partial(_inproj_kernel, slot=slot),
        grid=(n // tn, ntok // tm),
        in_specs=[pl.BlockSpec((tm, d), lambda j, i: (i, 0)),
                  pl.BlockSpec((1, N_MOD, d), lambda j, i: (mod_row(i), 0, 0)),
                  pl.BlockSpec((d, tn), lambda j, i: (0, j))],
        out_specs=pl.BlockSpec((tm, tn), lambda j, i: (i, j)),
        out_shape=jax.ShapeDtypeStruct((ntok, n), F32),
        compiler_params=_cparams(("parallel", "parallel")),
        name="inproj",
    )(x2, mods, w)


_PREP_NAMES = ("r", "v", "g", "kk", "kd0", "kd1", "ic0", "ic1", "cum0", "cum1")
_PREP_DTYPES = (BF16,) * 8 + (F32,) * 2


def _prep_kernel(*refs, grid_mode, tiles_per_seq, rw, slot):
    if grid_mode:
        p_ref, up_ref, dn_ref, x_ref, mod_ref, wgm_ref = refs[:6]
        refs = refs[6:]
    else:
        p_ref = refs[0]
        refs = refs[1:]
    (mu_ref, w0_ref, wup_ref, a0_ref, aup_ref, gup_ref, kk_ref, ka_ref, bd_ref, tri0_ref, tri1_ref,
     r_o, v_o, g_o, kk_o, kd0_o, kd1_o, ic0_o, ic1_o, cum0_o, cum1_o) = refs[:21]
    tm = p_ref.shape[0]
    n_gm = 4

    def gm_chunk(j):
        if grid_mode:
            pgm_o = refs[21]
            wn = pgm_o.shape[1] // n_gm
            pgm_o[:, j * wn:(j + 1) * wn] = _dot(h_gm, wgm_ref[:, j * wn:(j + 1) * wn]).astype(BF16)

    def mixed(lo, hi):
        p = p_ref[:, lo:hi]
        width = hi - lo
        row = jax.lax.broadcasted_iota(jnp.int32, (tm, width), 0)
        lane = jax.lax.broadcasted_iota(jnp.int32, (tm, width), 1)
        prev = pltpu.roll(p, 1, axis=0)
        nxt = pltpu.roll(p, tm - 1, axis=0)
        if grid_mode:
            i = pl.program_id(0) % tiles_per_seq
            col = row % GRID_W
            left = jnp.where(col > 0, prev, 0.0)
            right = jnp.where(col < GRID_W - 1, nxt, 0.0)
            up_halo = jnp.where(i > 0, up_ref[:, lo:hi], 0.0)
            dn_halo = jnp.where(i < tiles_per_seq - 1, dn_ref[:, lo:hi], 0.0)
            up = jnp.concatenate([up_halo, p[:tm - GRID_W]], axis=0)
            down = jnp.concatenate([p[GRID_W:], dn_halo], axis=0)
            c4 = lane % 4
            shifted = jnp.where(c4 == 0, left, jnp.where(c4 == 1, right, jnp.where(c4 == 2, up, down)))
        else:
            prev = jnp.where(row > 0, prev, 0.0)
            nxt = jnp.where(row < tm - 1, nxt, 0.0)
            shifted = jnp.where(lane % 2 == 0, prev, nxt)
        return p + (shifted - p) * mu_ref[:, lo:hi]

    if grid_mode:
        shift = mod_ref[0, 3 * slot:3 * slot + 1, :]
        scale = mod_ref[0, 3 * slot + 1:3 * slot + 2, :]
        h_gm = (x_ref[...] * (1.0 + scale) + shift).astype(BF16)
    wa_lo = mixed(3 * rw, 3 * rw + 128)
    g_lo = mixed(3 * rw + 128, 3 * rw + 256)
    tanh_wa = jnp.tanh(wa_lo).astype(BF16)
    wa_bf = wa_lo.astype(BF16)
    sig_g = _sigmoid(g_lo).astype(BF16)
    gm_chunk(0)
    k = mixed(rw, 2 * rw)
    kk = k * kk_ref[...]
    kk_sq = kk * kk
    g_lin = _dot(sig_g, gup_ref[...])
    z_w = [_dot(tanh_wa, wup_ref[d]) for d in range(2)]
    z_a = [_dot(wa_bf, aup_ref[d]) for d in range(2)]
    gm_chunk(1)
    ss = _head_sum(kk_sq, bd_ref[...])
    gm_chunk(2)
    g_o[...] = g_lin.astype(BF16)
    r_o[...] = mixed(0, rw).astype(BF16)
    v_o[...] = mixed(2 * rw, 3 * rw).astype(BF16)
    kk_o[...] = (kk / jnp.maximum(jnp.sqrt(ss), 1e-12)).astype(BF16)
    for d, (tri_ref, cum_o, kd_o, ic_o) in enumerate(((tri0_ref, cum0_o, kd0_o, ic0_o),
                                                      (tri1_ref, cum1_o, kd1_o, ic1_o))):
        lw = -math.exp(-0.5) * _sigmoid(w0_ref[d:d + 1, :] + z_w[d])
        hi = lw.astype(BF16)
        rem = lw - hi.astype(F32)
        mid = rem.astype(BF16)
        lo = (rem - mid.astype(F32)).astype(BF16)
        tri = tri_ref[...]
        cum_o[...] = _dot(tri, hi) + _dot(tri, mid) + _dot(tri, lo)
        iclr = _sigmoid(a0_ref[d:d + 1, :] + z_a[d])
        ic_o[...] = iclr.astype(BF16)
        kd_o[...] = (k * (1.0 + (iclr - 1.0) * ka_ref[...])).astype(BF16)
    gm_chunk(3)


def _prep(p_rw, params, bd, tokens_per_seq, tm, gm=None):
    ntok, width = p_rw.shape
    rw = params["w0"].shape[1]
    grid_mode = gm is not None
    tiles_per_seq = tokens_per_seq // tm
    hb = tm // GRID_W
    nhalo = ntok // GRID_W
    const = lambda a: pl.BlockSpec(a.shape, lambda i: (0,) * a.ndim, pipeline_mode=pl.Buffered(1))
    tok = lambda w: pl.BlockSpec((tm, w), lambda i: (i, 0))
    in_specs = [tok(width)]
    args = [p_rw]
    names, dtypes, widths = list(_PREP_NAMES), list(_PREP_DTYPES), [rw] * 10
    slot = 0
    if grid_mode:
        x2, mods, mod_row, w_gm, slot = gm
        d = x2.shape[1]
        in_specs += [pl.BlockSpec((GRID_W, width), lambda i: (jnp.maximum(i * hb - 1, 0), 0)),
                     pl.BlockSpec((GRID_W, width), lambda i: (jnp.minimum((i + 1) * hb, nhalo - 1), 0)),
                     tok(d), pl.BlockSpec((1, N_MOD, d), lambda i: (mod_row(i), 0, 0)), const(w_gm)]
        args += [p_rw, p_rw, x2, mods, w_gm]
        names.append("p_gm")
        dtypes.append(BF16)
        widths.append(w_gm.shape[1])
    ti = jnp.arange(tm)[:, None]
    tj = jnp.arange(tm)[None, :]
    same = (ti // SCAN_CHUNK) == (tj // SCAN_CHUNK)
    tri = [(same & (tj <= ti)).astype(BF16), (same & (tj >= ti)).astype(BF16)]
    plist = [params[k] for k in ("mu", "w0", "w_up", "a0", "a_up", "g_up", "k_k", "k_a")] + [bd] + tri
    in_specs += [const(a) for a in plist]
    outs = pl.pallas_call(
        functools.partial(_prep_kernel, grid_mode=grid_mode, tiles_per_seq=tiles_per_seq, rw=rw, slot=slot),
        grid=(ntok // tm,),
        in_specs=in_specs,
        out_specs=[tok(w) for w in widths],
        out_shape=[jax.ShapeDtypeStruct((ntok, w), dt) for w, dt in zip(widths, dtypes)],
        compiler_params=_cparams(("parallel",)),
        name="prep_grid" if grid_mode else "prep_seq",
    )(*args, *plist)
    return dict(zip(names, outs))


def _stackmask(z, lane_head):
    return jnp.concatenate([jnp.where(lane_head == h, z, jnp.zeros_like(z)) for h in range(HEADS_PER_GROUP)],
                           axis=0)


def _scan_prepare(units, with_y):
    c, l = SCAN_CHUNK, GROUP_LANES
    gc = HEADS_PER_GROUP * c
    n_lvl = int(math.log2(c))
    t = jax.lax.broadcasted_iota(jnp.int32, (c, gc), 0)
    tj = jax.lax.broadcasted_iota(jnp.int32, (c, gc), 1) % c
    lane_head = jax.lax.broadcasted_iota(jnp.int32, (c, l), 1) // RWKV_HEAD
    bd = functools.partial(_stackmask, lane_head=lane_head)
    eye = jnp.where(tj == t, 1.0, 0.0)

    for w in units:
        w.update(w["load"]())
        at = (-w["kk"] * w["dexc"]).astype(BF16)
        rt = (w["r"] * w["dinc"]).astype(BF16)
        bt = (w["b"] * w["dinv"]).astype(BF16)
        kt = (w["kd"] * w["dinv"]).astype(BF16)
        w["lhs"] = jnp.concatenate([at, rt], axis=0) if with_y else at
        w["a_all"] = _dot_nt(w["lhs"], jnp.concatenate([bd(bt), bd(kt)], axis=0))
        w["bk"] = jnp.concatenate([w["b"] * w["dend"], w["kd"] * w["dend"]], axis=0).astype(BF16)
    for w in units:
        strict = (tj > t) if w["reverse"] else (tj < t)
        a_ab = jnp.where(strict, w["a_all"][:c, :gc], 0.0)
        a_v = jnp.where(strict, w["a_all"][:c, gc:], 0.0)
        if with_y:
            incl = (tj >= t) if w["reverse"] else (tj <= t)
            a_v = jnp.concatenate([a_v, jnp.where(incl, w["a_all"][c:, gc:], 0.0)], axis=0)
            w["a_rb"] = jnp.where(incl, w["a_all"][c:, :gc], 0.0).astype(BF16)
        w["from_v"] = _dot(a_v.astype(BF16), bd(w["v"].astype(BF16)))
        w["p"] = eye + a_ab
        w["apow"] = a_ab.astype(BF16)
    for w in units:
        w["apow"] = _dot(w["apow"], bd(w["apow"])).astype(BF16)
    for lvl in range(1, n_lvl):
        last = lvl == n_lvl - 1
        for w in units:
            p_bf = w["p"].astype(BF16)
            if last:
                w["p"] = (w["p"] + _dot(p_bf, bd(w["apow"]))).astype(BF16)
            else:
                both = _dot(jnp.concatenate([p_bf, w["apow"]], axis=0), bd(w["apow"]))
                w["p"] = w["p"] + both[:c]
                w["apow"] = both[c:].astype(BF16)


def _scan_apply(units, with_y):
    c, l = SCAN_CHUNK, GROUP_LANES
    lane_head = jax.lax.broadcasted_iota(jnp.int32, (c, l), 1) // RWKV_HEAD
    bd = functools.partial(_stackmask, lane_head=lane_head)
    rh = jax.lax.broadcasted_iota(jnp.int32, (l, l), 0) // RWKV_HEAD
    ch = jax.lax.broadcasted_iota(jnp.int32, (l, l), 1) // RWKV_HEAD
    for w in units:
        w["from_s"] = _dot_nt(w["lhs"], w["s"].astype(BF16))
    for w in units:
        u = w["from_s"][:c] + w["from_v"][:c]
        w["sa"] = _dot(w["p"], bd(u.astype(BF16)))
    outs = []
    for w in units:
        y = None
        if with_y:
            y = w["from_s"][c:] + w["from_v"][c:] + _dot(w["a_rb"], bd(w["sa"].astype(BF16)))
        sav_t = jnp.concatenate([w["sa"], w["v"]], axis=0).T.astype(BF16)
        upd = _dot(sav_t, w["bk"])
        outs.append((w["s"] * w["dtot"] + jnp.where(rh == ch, upd, 0.0), y))
    return outs


def _scan_kernel(*refs, with_y):
    ins = refs[:13]
    s0_ref = ins[12]
    if with_y:
        yf_ref, yb_ref, sfin_ref, s_sc = refs[13:]
        y_refs = (yf_ref, yb_ref)
    else:
        sfin_ref, s_sc = refs[13:]
        y_refs = (None, None)
    ci = pl.program_id(1)

    @pl.when(ci == 0)
    def _():
        s_sc[...] = s0_ref[...]

    c = SCAN_CHUNK
    n_sub = ins[0].shape[1] // c
    steps = [[] for _ in range(n_sub)]
    for n in range(s_sc.shape[0]):
        for d in range(2):
            r_ref, v_ref, kk_ref, cum_ref, kd_ref, ic_ref = ins[6 * d:6 * d + 6]
            reverse = d == 1
            for k in range(n_sub):
                rows = pl.ds((n_sub - 1 - k if reverse else k) * c, c)

                def load(n=n, reverse=reverse, r_ref=r_ref, v_ref=v_ref, kk_ref=kk_ref, cum_ref=cum_ref,
                         kd_ref=kd_ref, ic_ref=ic_ref, rows=rows, sl=None):
                    cum = cum_ref[n, rows, sl]
                    row = jax.lax.broadcasted_iota(jnp.int32, cum.shape, 0)
                    if reverse:
                        cum_excl = jnp.where(row == c - 1, 0.0, pltpu.roll(cum, c - 1, axis=0))
                        tot = cum[0:1, :]
                    else:
                        cum_excl = jnp.where(row == 0, 0.0, pltpu.roll(cum, 1, axis=0))
                        tot = cum[c - 1:c, :]
                    kk = kk_ref[n, rows, sl].astype(F32)
                    return dict(r=r_ref[n, rows, sl].astype(F32), v=v_ref[n, rows, sl].astype(F32), kk=kk,
                                kd=kd_ref[n, rows, sl].astype(F32), b=kk * ic_ref[n, rows, sl].astype(F32),
                                dinc=jnp.exp(cum), dexc=jnp.exp(cum_excl), dinv=jnp.exp(-cum),
                                dend=jnp.exp(tot - cum), dtot=jnp.exp(tot))

                for g in range(r_ref.shape[-1] // GROUP_LANES):
                    sl = slice(g * GROUP_LANES, (g + 1) * GROUP_LANES)
                    steps[k].append(dict(load=functools.partial(load, sl=sl), reverse=reverse, n=n, d=d, g=g,
                                         sl=sl, rows=rows))

    _scan_prepare([w for units in steps for w in units], with_y)
    state = {}
    for units in steps:
        for w in units:
            key = (w["n"], w["d"], w["g"])
            w["s"] = state[key] if key in state else s_sc[key]
        for w, (s_new, y) in zip(units, _scan_apply(units, with_y)):
            state[(w["n"], w["d"], w["g"])] = s_new
            if with_y:
                y_refs[w["d"]][w["n"], w["rows"], w["sl"]] = y
    for key, s_new in state.items():
        s_sc[key] = s_new

    @pl.when(ci == pl.num_programs(1) - 1)
    def _():
        sfin_ref[...] = s_sc[...]


def _scan(q, s0, with_y, n_sub):
    bsz, t, rw = q["r"].shape
    c = SCAN_CHUNK * n_sub
    assert t % c == 0
    nb = SCAN_BATCH if bsz % SCAN_BATCH == 0 else 1
    nc = t // c
    ng = rw // GROUP_LANES
    fwd = pl.BlockSpec((nb, c, rw), lambda b, i: (b, i, 0))
    bwd = pl.BlockSpec((nb, c, rw), lambda b, i: (b, nc - 1 - i, 0))
    s_spec = pl.BlockSpec((nb, 2, ng, GROUP_LANES, GROUP_LANES), lambda b, i: (b, 0, 0, 0, 0))
    args = [q["r"], q["v"], q["kk"], q["cum0"], q["kd0"], q["ic0"],
            q["r"], q["v"], q["kk"], q["cum1"], q["kd1"], q["ic1"], s0]
    in_specs = [fwd] * 6 + [bwd] * 6 + [s_spec]
    y_shape = jax.ShapeDtypeStruct((bsz, t, rw), F32)
    out_specs = ([fwd, bwd] if with_y else []) + [s_spec]
    out_shape = ([y_shape, y_shape] if with_y else []) + [jax.ShapeDtypeStruct(s0.shape, F32)]
    outs = pl.pallas_call(
        functools.partial(_scan_kernel, with_y=with_y),
        grid=(bsz // nb, nc),
        in_specs=in_specs,
        out_specs=out_specs,
        out_shape=out_shape,
        scratch_shapes=[pltpu.VMEM((nb, 2, ng, GROUP_LANES, GROUP_LANES), F32)],
        compiler_params=_cparams(("parallel", "arbitrary")),
        name="scan_y" if with_y else "scan_state",
    )(*args)
    return outs


def _mixout_kernel(yf_ref, yb_ref, r_ref, v_ref, g_ref, kd0_ref, kd1_ref, gm_ref, x_ref, mod_ref,
                   rk_ref, gng_ref, gnb_ref, glg_ref, glb_ref, ws_ref, bs_ref, wout_ref, lng_ref, lnb_ref,
                   bd_ref, o_ref, *, slot):
    bd = bd_ref[...]
    rw = yf_ref.shape[-1]
    inv_n = 1.0 / RWKV_HEAD
    inv_g = 1.0 / GMLP_GROUP
    f32 = lambda ref: ref[...].astype(F32)
    y = yf_ref[...] + yb_ref[...]
    vv = _gelu_tanh(gm_ref[:, rw:].astype(F32))
    mu = _head_sum(y, bd) * inv_n
    mu_v = _head_sum(vv, bd) * inv_g
    bonus_in = f32(r_ref) * (f32(kd0_ref) + f32(kd1_ref)) * rk_ref[...]
    dy = y - mu
    var = _head_sum(dy * dy, bd) * inv_n
    dv = vv - mu_v
    var_v = _head_sum(dv * dv, bd) * inv_g
    bonus = _head_sum(bonus_in, bd) * f32(v_ref)
    yn = dy * jax.lax.rsqrt(var + GN_EPS) * gng_ref[...] + gnb_ref[...]
    out_r = ((yn + bonus) * f32(g_ref)).astype(BF16)
    vn = (dv * jax.lax.rsqrt(var_v + LN_EPS) * glg_ref[...] + glb_ref[...]).astype(BF16)
    o_r = _dot(out_r, wout_ref[0:rw, :])
    u = _gelu_tanh(gm_ref[:, :rw].astype(F32))
    tm = vn.shape[0]
    lane = jax.lax.broadcasted_iota(jnp.int32, (CHUNK, 2 * GMLP_GROUP), 1)
    chunks = []
    for n in range(tm // CHUNK):
        pairs = []
        for gp in range(rw // (2 * GMLP_GROUP)):
            v2 = vn[n * CHUNK:(n + 1) * CHUNK, gp * 2 * GMLP_GROUP:(gp + 1) * 2 * GMLP_GROUP]
            m0 = _dot(ws_ref[2 * gp], v2)
            m1 = _dot(ws_ref[2 * gp + 1], v2)
            pairs.append(jnp.where(lane < GMLP_GROUP, m0, m1))
        chunks.append(jnp.concatenate(pairs, axis=1) + bs_ref[...])
    mixed = jnp.concatenate(chunks, axis=0)
    out_g = (u * mixed).astype(BF16)

    o = o_r + _dot(out_g, wout_ref[rw:, :])
    gate = mod_ref[0, 3 * slot + 2:3 * slot + 3, :]
    z = ALPHA * x_ref[...] + gate * o
    o_ref[...] = _layer_norm(z, lng_ref[...], lnb_ref[...])


def _mixout(yf, yb, q, p_gm, x2, mods, mod_row, params, bd, slot, tm):
    ntok, d = x2.shape
    rw = yf.shape[-1]
    tok = lambda w: pl.BlockSpec((tm, w), lambda i: (i, 0))
    const = lambda a: pl.BlockSpec(a.shape, lambda i: (0,) * a.ndim, pipeline_mode=pl.Buffered(1))
    plist = [params[k] for k in ("r_k", "gn_g", "gn_b", "gm_ln_g", "gm_ln_b", "gm_ws", "gm_bs_full", "w_out",
                                 "ln_g", "ln_b")] + [bd]
    return pl.pallas_call(
        functools.partial(_mixout_kernel, slot=slot),
        grid=(ntok // tm,),
        in_specs=[tok(rw)] * 7 + [tok(p_gm.shape[1]), tok(d),
                                  pl.BlockSpec((1, N_MOD, d), lambda i: (mod_row(i), 0, 0))]
                 + [const(a) for a in plist],
        out_specs=tok(d),
        out_shape=jax.ShapeDtypeStruct((ntok, d), F32),
        compiler_params=_cparams(("parallel",)),
        name="mixout",
    )(yf, yb, q["r"], q["v"], q["g"], q["kd0"], q["kd1"], p_gm, x2, mods, *plist)


def kernel(x, c, ctx, c_ctx, w_ada, b_ada, ln_g, ln_b, ffn_a_wi, ffn_a_wo, ffn_b_wi, ffn_b_wo, w_in, mu_shift,
           w0, w_up, a0, a_up, g_up, k_k, k_a, r_k, gn_g, gn_b, gm_ln_g, gm_ln_b, gm_ws, gm_bs, w_out):
    bsz, seq, d = x.shape
    ctx_len = ctx.shape[1]
    assert w_ada.shape[0] == DEPTH
    rw = w0.shape[-1]
    rwkv_in = mu_shift.shape[-1]
    w_lora, a_lora = w_up.shape[2], a_up.shape[2]
    assert w_lora + a_lora == 128 and g_up.shape[1] == 128 and rwkv_in == 3 * rw + 256
    assert seq % GRID_W == 0 and seq % SCAN_CHUNK == 0 and ctx_len % SCAN_CHUNK == 0 and seq % CHUNK == 0
    i = 0

    n_rows = -(-(bsz + 1) // 8) * 8
    cc = jnp.zeros((n_rows, d), F32).at[:bsz].set(c).at[bsz].set(c_ctx)
    mods = _ada(cc, w_ada[i], b_ada[i]).reshape(n_rows, N_MOD, d)

    x2 = x.reshape(bsz * seq, d)
    c2 = ctx.reshape(bsz * ctx_len, d)
    tm_fx = _pick(seq, FFN_TM)
    tm_fc = _pick(bsz * ctx_len, FFN_TM)
    row_x = lambda tm: (lambda t: t // (seq // tm))
    row_c = lambda t: bsz

    bf = lambda a: a.astype(BF16)
    wi_a, wo_a = _ffn_weights(ffn_a_wi[i], ffn_a_wo[i])
    x1 = _ffn(x2, mods, row_x(tm_fx), wi_a, wo_a, ln_g[i, 0], ln_b[i, 0], 0, tm_fx)
    c1 = _ffn(c2, mods, row_c, wi_a, wo_a, ln_g[i, 0], ln_b[i, 0], 0, tm_fc)

    w_rw, w_gm = bf(w_in[i, :, :rwkv_in]), bf(w_in[i, :, rwkv_in:])
    p_rw_x = _inproj(x1, mods, row_x(tm_fx), w_rw, 1, tm_fx, rwkv_in // 2)
    p_rw_c = _inproj(c1, mods, row_c, w_rw, 1, tm_fc, rwkv_in // 2)

    zpad = lambda a, lo, hi: jnp.pad(a, ((0, 0), (lo, hi), (0, 0)))
    lanes = jnp.arange(GROUP_LANES) // RWKV_HEAD
    bd = (lanes[:, None] == lanes[None, :]).astype(BF16)
    prm = dict(mu=mu_shift[i].reshape(1, rwkv_in), w0=w0[i], w_up=bf(zpad(w_up[i], 0, a_lora)), a0=a0[i],
               a_up=bf(zpad(a_up[i], w_lora, 0)), g_up=bf(g_up[i]), k_k=k_k[i].reshape(1, rw),
               k_a=k_a[i].reshape(1, rw))
    tm_p = _pick(seq, 256)
    q_x = _prep(p_rw_x, prm, bd, seq, tm_p, gm=(x1, mods, row_x(tm_p), w_gm, 1))
    q_c = _prep(p_rw_c, prm, bd, ctx_len, ctx_len)
    p_gm_x = q_x.pop("p_gm")
    q_x3 = {k: a.reshape(bsz, seq, rw) for k, a in q_x.items()}
    q_c3 = {k: a.reshape(bsz, ctx_len, rw) for k, a in q_c.items()}

    ng = rw // GROUP_LANES
    s_zero = jnp.zeros((bsz, 2, ng, GROUP_LANES, GROUP_LANES), F32)
    (s_ctx,) = _scan(q_c3, s_zero, with_y=False, n_sub=1)
    yf, yb, _ = _scan(q_x3, s_ctx, with_y=True, n_sub=SCAN_SUB)

    out_prm = dict(r_k=r_k[i].reshape(1, rw), gn_g=gn_g[i].reshape(1, rw), gn_b=gn_b[i].reshape(1, rw),
                   gm_ln_g=gm_ln_g[i].reshape(1, rw), gm_ln_b=gm_ln_b[i].reshape(1, rw), gm_ws=bf(gm_ws[i]),
                   gm_bs_full=jnp.repeat(gm_bs[i].T, GMLP_GROUP, axis=1), w_out=bf(w_out[i]),
                   ln_g=ln_g[i, 1].reshape(1, d), ln_b=ln_b[i, 1].reshape(1, d))
    tm_o = _pick(seq, 256)
    x2b = _mixout(yf.reshape(bsz * seq, rw), yb.reshape(bsz * seq, rw), q_x, p_gm_x, x1, mods, row_x(tm_o),
                  out_prm, bd, 1, tm_o)

    wi_b, wo_b = _ffn_weights(ffn_b_wi[i], ffn_b_wo[i])
    x3 = _ffn(x2b, mods, row_x(tm_fx), wi_b, wo_b, ln_g[i, 2], ln_b[i, 2], 2, tm_fx)
    return x3.reshape(bsz, seq, d)
```

```python
import functools
import math

import jax
import jax.numpy as jnp
from jax.experimental import pallas as pl
from jax.experimental.pallas import tpu as pltpu

F32 = jnp.float32
BF16 = jnp.bfloat16

GRID_W = 64
RWKV_HEAD = 64
CHUNK = 128
GMLP_GROUP = 64
N_MOD = 9
LN_EPS = 1e-5
GN_EPS = 64e-5
DEPTH = 1
ALPHA = (2.0 * DEPTH) ** 0.25

SCAN_CHUNK = 64
HEADS_PER_GROUP = 4
GROUP_LANES = HEADS_PER_GROUP * RWKV_HEAD
SCAN_BATCH = 2
SCAN_SUB = 2
VMEM_LIMIT = 56 * 1024 * 1024


def _cparams(sem, vmem_limit=VMEM_LIMIT):
    return pltpu.CompilerParams(dimension_semantics=sem, vmem_limit_bytes=vmem_limit)


def _pick(n, pref):
    t = min(n, pref)
    while n % t:
        t -= 64
    return t


def _sigmoid(z):
    return 1.0 / (1.0 + jnp.exp(-z))


def _silu(z):
    return z * _sigmoid(z)


def _gelu_tanh(z):
    return 0.5 * z * (1.0 + jnp.tanh(math.sqrt(2.0 / math.pi) * (z + 0.044715 * (z * z * z))))


def _layer_norm(z, g, b):
    mu = jnp.mean(z, axis=-1, keepdims=True)
    d = z - mu
    var = jnp.mean(d * d, axis=-1, keepdims=True)
    return d * jax.lax.rsqrt(var + LN_EPS) * g + b


def _dot(a, b):
    return jnp.dot(a, b, preferred_element_type=F32)


def _dot_nt(a, b):
    return jax.lax.dot_general(a, b, (((1,), (1,)), ((), ())), preferred_element_type=F32)


def _group_sum(z, bd):
    return _dot(z.astype(BF16), bd)


def _head_sum(z, bd):
    w = z.shape[-1]
    return jnp.concatenate([_group_sum(z[:, s:s + GROUP_LANES], bd) for s in range(0, w, GROUP_LANES)], axis=1)


def _ada_kernel(c_ref, w_ref, b_ref, o_ref):
    a = _silu(c_ref[...]).astype(BF16)
    o_ref[...] = _dot(a, w_ref[...].astype(BF16)) + b_ref[...]


def _ada(cc, w_ada, b_ada):
    m, d = cc.shape
    n = w_ada.shape[1]
    tn = 1024
    return pl.pallas_call(
        _ada_kernel,
        grid=(n // tn,),
        in_specs=[pl.BlockSpec((m, d), lambda j: (0, 0)),
                  pl.BlockSpec((d, tn), lambda j: (0, j)),
                  pl.BlockSpec((1, tn), lambda j: (0, j))],
        out_specs=pl.BlockSpec((m, tn), lambda j: (0, j)),
        out_shape=jax.ShapeDtypeStruct((m, n), F32),
        compiler_params=_cparams(("arbitrary",)),
        name="ada",
    )(cc, w_ada, b_ada.reshape(1, n))


def _ffn_kernel(x_ref, mod_ref, wig_ref, wiu_ref, wo_ref, g_ref, b_ref, o_ref, h_sc, *, slot):
    f = pl.program_id(1)

    @pl.when(f == 0)
    def _():
        shift = mod_ref[0, 3 * slot:3 * slot + 1, :]
        scale = mod_ref[0, 3 * slot + 1:3 * slot + 2, :]
        h_sc[...] = (x_ref[...] * (1.0 + scale) + shift).astype(BF16)
        o_ref[...] = jnp.zeros_like(o_ref)

    half = wo_ref.shape[0] // 2
    cols = [slice(j * half, (j + 1) * half) for j in range(2)]
    n_rows = min(FFN_ROWS, h_sc.shape[0])
    for r0 in range(0, h_sc.shape[0], n_rows):
        rows = slice(r0, r0 + n_rows)
        h = h_sc[rows, :]
        gate_up = [(_dot(h, wig_ref[0, :, c]), _dot(h, wiu_ref[0, :, c])) for c in cols]
        out = None
        for c, (gate, up) in zip(cols, gate_up):
            part = _dot((_silu(gate) * up).astype(BF16), wo_ref[c, :])
            out = part if out is None else out + part
        o_ref[rows, :] += out

    @pl.when(f == pl.num_programs(1) - 1)
    def _():
        gmod = mod_ref[0, 3 * slot + 2:3 * slot + 3, :]
        z = ALPHA * x_ref[...] + 0.5 * gmod * o_ref[...]
        o_ref[...] = _layer_norm(z, g_ref[...], b_ref[...])


FFN_TF = 512
FFN_TM = 1024
FFN_ROWS = 512
FFN_VMEM_LIMIT = 62 * 1024 * 1024


def _cast_tiles_kernel(w_ref, o_ref):
    o_ref[0] = w_ref[...].astype(BF16)


def _cast_col_tiles(w, tn):
    d, n = w.shape
    return pl.pallas_call(
        _cast_tiles_kernel,
        grid=(n // tn,),
        in_specs=[pl.BlockSpec((d, tn), lambda j: (0, j))],
        out_specs=pl.BlockSpec((1, d, tn), lambda j: (j, 0, 0)),
        out_shape=jax.ShapeDtypeStruct((n // tn, d, tn), BF16),
        compiler_params=_cparams(("parallel",)),
        name="cast_tiles",
    )(w)


def _ffn_weights(wi, wo):
    return _cast_col_tiles(wi, FFN_TF), wo.astype(BF16)


def _ffn(x2, mods, mod_row, wi, wo, ln_g, ln_b, slot, tm):
    ntok, d = x2.shape
    dff = wo.shape[0]
    tf = FFN_TF
    nf = dff // tf
    return pl.pallas_call(
        functools.partial(_ffn_kernel, slot=slot),
        grid=(ntok // tm, nf),
        in_specs=[pl.BlockSpec((tm, d), lambda i, f: (i, 0)),
                  pl.BlockSpec((1, N_MOD, d), lambda i, f: (mod_row(i), 0, 0)),
                  pl.BlockSpec((1, d, tf), lambda i, f: (f, 0, 0)),
                  pl.BlockSpec((1, d, tf), lambda i, f: (nf + f, 0, 0)),
                  pl.BlockSpec((tf, d), lambda i, f: (f, 0)),
                  pl.BlockSpec((1, d), lambda i, f: (0, 0)),
                  pl.BlockSpec((1, d), lambda i, f: (0, 0))],
        out_specs=pl.BlockSpec((tm, d), lambda i, f: (i, 0)),
        out_shape=jax.ShapeDtypeStruct((ntok, d), F32),
        scratch_shapes=[pltpu.VMEM((tm, d), BF16)],
        compiler_params=_cparams(("parallel", "arbitrary"), FFN_VMEM_LIMIT),
        name=f"ffn{slot}",
    )(x2, mods, wi, wi, wo, ln_g.reshape(1, d), ln_b.reshape(1, d))


def _inproj_kernel(x_ref, mod_ref, w_ref, o_ref, *, slot):
    shift = mod_ref[0, 3 * slot:3 * slot + 1, :]
    scale = mod_ref[0, 3 * slot + 1:3 * slot + 2, :]
    h = (x_ref[...] * (1.0 + scale) + shift).astype(BF16)
    o_ref[...] = _dot(h, w_ref[...])


def _inproj(x2, mods, mod_row, w, slot, tm, tn):
    ntok, d = x2.shape
    n = w.shape[1]
    return pl.pallas_call(
        functools.partial(_inproj_kernel, slot=slot),
        grid=(n // tn, ntok // tm),
        in_specs=[pl.BlockSpec((tm, d), lambda j, i: (i, 0)),
                  pl.BlockSpec((1, N_MOD, d), lambda j, i: (mod_row(i), 0, 0)),
                  pl.BlockSpec((d, tn), lambda j, i: (0, j))],
        out_specs=pl.BlockSpec((tm, tn), lambda j, i: (i, j)),
        out_shape=jax.ShapeDtypeStruct((ntok, n), F32),
        compiler_params=_cparams(("parallel", "parallel")),
        name="inproj",
    )(x2, mods, w)


_PREP_NAMES = ("r", "v", "g", "kk", "kd0", "kd1", "ic0", "ic1", "cum0", "cum1")
_PREP_DTYPES = (BF16,) * 8 + (F32,) * 2


def _prep_kernel(*refs, grid_mode, tiles_per_seq, rw, slot):
    if grid_mode:
        p_ref, up_ref, dn_ref, x_ref, mod_ref, wgm_ref = refs[:6]
        refs = refs[6:]
    else:
        p_ref = refs[0]
        refs = refs[1:]
    (mu_ref, w0_ref, wup_ref, a0_ref, aup_ref, gup_ref, kk_ref, ka_ref, bd_ref, tri0_ref, tri1_ref,
     r_o, v_o, g_o, kk_o, kd0_o, kd1_o, ic0_o, ic1_o, cum0_o, cum1_o) = refs[:21]
    tm = p_ref.shape[0]
    n_gm = 4

    def gm_chunk(j):
        if grid_mode:
            pgm_o = refs[21]
            wn = pgm_o.shape[1] // n_gm
            pgm_o[:, j * wn:(j + 1) * wn] = _dot(h_gm, wgm_ref[:, j * wn:(j + 1) * wn]).astype(BF16)

    def mixed(lo, hi):
        p = p_ref[:, lo:hi]
        width = hi - lo
        row = jax.lax.broadcasted_iota(jnp.int32, (tm, width), 0)
        lane = jax.lax.broadcasted_iota(jnp.int32, (tm, width), 1)
        prev = pltpu.roll(p, 1, axis=0)
        nxt = pltpu.roll(p, tm - 1, axis=0)
        if grid_mode:
            i = pl.program_id(0) % tiles_per_seq
            col = row % GRID_W
            left = jnp.where(col > 0, prev, 0.0)
            right = jnp.where(col < GRID_W - 1, nxt, 0.0)
            up_halo = jnp.where(i > 0, up_ref[:, lo:hi], 0.0)
            dn_halo = jnp.where(i < tiles_per_seq - 1, dn_ref[:, lo:hi], 0.0)
            up = jnp.concatenate([up_halo, p[:tm - GRID_W]], axis=0)
            down = jnp.concatenate([p[GRID_W:], dn_halo], axis=0)
            c4 = lane % 4
            shifted = jnp.where(c4 == 0, left, jnp.where(c4 == 1, right, jnp.where(c4 == 2, up, down)))
        else:
            prev = jnp.where(row > 0, prev, 0.0)
            nxt = jnp.where(row < tm - 1, nxt, 0.0)
            shifted = jnp.where(lane % 2 == 0, prev, nxt)
        return p + (shifted - p) * mu_ref[:, lo:hi]

    if grid_mode:
        shift = mod_ref[0, 3 * slot:3 * slot + 1, :]
        scale = mod_ref[0, 3 * slot + 1:3 * slot + 2, :]
        h_gm = (x_ref[...] * (1.0 + scale) + shift).astype(BF16)
    wa_lo = mixed(3 * rw, 3 * rw + 128)
    g_lo = mixed(3 * rw + 128, 3 * rw + 256)
    tanh_wa = jnp.tanh(wa_lo).astype(BF16)
    wa_bf = wa_lo.astype(BF16)
    sig_g = _sigmoid(g_lo).astype(BF16)
    gm_chunk(0)
    k = mixed(rw, 2 * rw)
    kk = k * kk_ref[...]
    kk_sq = kk * kk
    g_lin = _dot(sig_g, gup_ref[...])
    z_w = [_dot(tanh_wa, wup_ref[d]) for d in range(2)]
    z_a = [_dot(wa_bf, aup_ref[d]) for d in range(2)]
    gm_chunk(1)
    ss = _head_sum(kk_sq, bd_ref[...])
    gm_chunk(2)
    g_o[...] = g_lin.astype(BF16)
    r_o[...] = mixed(0, rw).astype(BF16)
    v_o[...] = mixed(2 * rw, 3 * rw).astype(BF16)
    kk_o[...] = (kk / jnp.maximum(jnp.sqrt(ss), 1e-12)).astype(BF16)
    for d, (tri_ref, cum_o, kd_o, ic_o) in enumerate(((tri0_ref, cum0_o, kd0_o, ic0_o),
                                                      (tri1_ref, cum1_o, kd1_o, ic1_o))):
        lw = -math.exp(-0.5) * _sigmoid(w0_ref[d:d + 1, :] + z_w[d])
        hi = lw.astype(BF16)
        rem = lw - hi.astype(F32)
        mid = rem.astype(BF16)
        lo = (rem - mid.astype(F32)).astype(BF16)
        tri = tri_ref[...]
        cum_o[...] = _dot(tri, hi) + _dot(tri, mid) + _dot(tri, lo)
        iclr = _sigmoid(a0_ref[d:d + 1, :] + z_a[d])
        ic_o[...] = iclr.astype(BF16)
        kd_o[...] = (k * (1.0 + (iclr - 1.0) * ka_ref[...])).astype(BF16)
    gm_chunk(3)


def _prep(p_rw, params, bd, tokens_per_seq, tm, gm=None):
    ntok, width = p_rw.shape
    rw = params["w0"].shape[1]
    grid_mode = gm is not None
    tiles_per_seq = tokens_per_seq // tm
    hb = tm // GRID_W
    nhalo = ntok // GRID_W
    const = lambda a: pl.BlockSpec(a.shape, lambda i: (0,) * a.ndim, pipeline_mode=pl.Buffered(1))
    tok = lambda w: pl.BlockSpec((tm, w), lambda i: (i, 0))
    in_specs = [tok(width)]
    args = [p_rw]
    names, dtypes, widths = list(_PREP_NAMES), list(_PREP_DTYPES), [rw] * 10
    slot = 0
    if grid_mode:
        x2, mods, mod_row, w_gm, slot = gm
        d = x2.shape[1]
        in_specs += [pl.BlockSpec((GRID_W, width), lambda i: (jnp.maximum(i * hb - 1, 0), 0)),
                     pl.BlockSpec((GRID_W, width), lambda i: (jnp.minimum((i + 1) * hb, nhalo - 1), 0)),
                     tok(d), pl.BlockSpec((1, N_MOD, d), lambda i: (mod_row(i), 0, 0)), const(w_gm)]
        args += [p_rw, p_rw, x2, mods, w_gm]
        names.append("p_gm")
        dtypes.append(BF16)
        widths.append(w_gm.shape[1])
    ti = jnp.arange(tm)[:, None]
    tj = jnp.arange(tm)[None, :]
    same = (ti // SCAN_CHUNK) == (tj // SCAN_CHUNK)
    tri = [(same & (tj <= ti)).astype(BF16), (same & (tj >= ti)).astype(BF16)]
    plist = [params[k] for k in ("mu", "w0", "w_up", "a0", "a_up", "g_up", "k_k", "k_a")] + [bd] + tri
    in_specs += [const(a) for a in plist]
    outs = pl.pallas_call(
        functools.partial(_prep_kernel, grid_mode=grid_mode, tiles_per_seq=tiles_per_seq, rw=rw, slot=slot),
        grid=(ntok // tm,),
        in_specs=in_specs,
        out_specs=[tok(w) for w in widths],
        out_shape=[jax.ShapeDtypeStruct((ntok, w), dt) for w, dt in zip(widths, dtypes)],
        compiler_params=_cparams(("parallel",)),
        name="prep_grid" if grid_mode else "prep_seq",
    )(*args, *plist)
    return dict(zip(names, outs))


def _stackmask(z, lane_head):
    return jnp.concatenate([jnp.where(lane_head == h, z, jnp.zeros_like(z)) for h in range(HEADS_PER_GROUP)],
                           axis=0)


def _scan_prepare(units, with_y):
    c, l = SCAN_CHUNK, GROUP_LANES
    gc = HEADS_PER_GROUP * c
    n_lvl = int(math.log2(c))
    t = jax.lax.broadcasted_iota(jnp.int32, (c, gc), 0)
    tj = jax.lax.broadcasted_iota(jnp.int32, (c, gc), 1) % c
    lane_head = jax.lax.broadcasted_iota(jnp.int32, (c, l), 1) // RWKV_HEAD
    lane_head2 = jax.lax.broadcasted_iota(jnp.int32, (2 * c, l), 1) // RWKV_HEAD
    bd = functools.partial(_stackmask, lane_head=lane_head)
    eye = jnp.where(tj == t, 1.0, 0.0)

    for w in units:
        w.update(w["load"]())
        at = (-w["kk"] * w["dexc"]).astype(BF16)
        rt = (w["r"] * w["dinc"]).astype(BF16)
        bt = (w["b"] * w["dinv"]).astype(BF16)
        kt = (w["kd"] * w["dinv"]).astype(BF16)
        w["lhs"] = jnp.concatenate([at, rt], axis=0) if with_y else at
        w["a_all"] = _dot_nt(w["lhs"], jnp.concatenate([bd(bt), bd(kt)], axis=0))
        bk = jnp.concatenate([w["b"] * w["dend"], w["kd"] * w["dend"]], axis=0).astype(BF16)
        w["bk"] = _stackmask(bk, lane_head2)
    for w in units:
        strict = (tj > t) if w["reverse"] else (tj < t)
        a_ab = jnp.where(strict, w["a_all"][:c, :gc], 0.0)
        a_v = jnp.where(strict, w["a_all"][:c, gc:], 0.0)
        if with_y:
            incl = (tj >= t) if w["reverse"] else (tj <= t)
            a_v = jnp.concatenate([a_v, jnp.where(incl, w["a_all"][c:, gc:], 0.0)], axis=0)
            w["a_rb"] = jnp.where(incl, w["a_all"][c:, :gc], 0.0).astype(BF16)
        w["from_v"] = _dot(a_v.astype(BF16), bd(w["v"].astype(BF16)))
        w["p"] = eye + a_ab
        w["apow"] = a_ab.astype(BF16)
    for w in units:
        w["apow"] = _dot(w["apow"], bd(w["apow"])).astype(BF16)
    for lvl in range(1, n_lvl):
        last = lvl == n_lvl - 1
        for w in units:
            p_bf = w["p"].astype(BF16)
            if last:
                w["p"] = (w["p"] + _dot(p_bf, bd(w["apow"]))).astype(BF16)
            else:
                both = _dot(jnp.concatenate([p_bf, w["apow"]], axis=0), bd(w["apow"]))
                w["p"] = w["p"] + both[:c]
                w["apow"] = both[c:].astype(BF16)


def _scan_apply(units, with_y):
    c, l = SCAN_CHUNK, GROUP_LANES
    lane_head = jax.lax.broadcasted_iota(jnp.int32, (c, l), 1) // RWKV_HEAD
    bd = functools.partial(_stackmask, lane_head=lane_head)
    for w in units:
        w["from_s"] = _dot_nt(w["lhs"], bd(w["s"].astype(BF16)))
    for w in units:
        u = w["from_s"][:c] + w["from_v"][:c]
        w["sa"] = _dot(w["p"], bd(u.astype(BF16)))
    outs = []
    for w in units:
        y = None
        if with_y:
            y = w["from_s"][c:] + w["from_v"][c:] + _dot(w["a_rb"], bd(w["sa"].astype(BF16)))
        sav_t = jnp.concatenate([w["sa"], w["v"]], axis=0).T.astype(BF16)
        sav_side = jnp.concatenate([sav_t[h * RWKV_HEAD:(h + 1) * RWKV_HEAD, :] for h in range(HEADS_PER_GROUP)],
                                   axis=1)
        outs.append((w["s"] * w["dtot"] + _dot(sav_side, w["bk"]), y))
    return outs


def _scan_kernel(*refs, with_y):
    ins = refs[:13]
    s0_ref = ins[12]
    if with_y:
        yf_ref, yb_ref, sfin_ref, s_sc = refs[13:]
        y_refs = (yf_ref, yb_ref)
    else:
        sfin_ref, s_sc = refs[13:]
        y_refs = (None, None)
    ci = pl.program_id(1)

    @pl.when(ci == 0)
    def _():
        s_sc[...] = s0_ref[...]

    c = SCAN_CHUNK
    n_sub = ins[0].shape[1] // c
    steps = [[] for _ in range(n_sub)]
    for n in range(s_sc.shape[0]):
        for d in range(2):
            r_ref, v_ref, kk_ref, cum_ref, kd_ref, ic_ref = ins[6 * d:6 * d + 6]
            reverse = d == 1
            for k in range(n_sub):
                rows = pl.ds((n_sub - 1 - k if reverse else k) * c, c)

                def load(n=n, reverse=reverse, r_ref=r_ref, v_ref=v_ref, kk_ref=kk_ref, cum_ref=cum_ref,
                         kd_ref=kd_ref, ic_ref=ic_ref, rows=rows, sl=None):
                    cum = cum_ref[n, rows, sl]
                    row = jax.lax.broadcasted_iota(jnp.int32, cum.shape, 0)
                    if reverse:
                        cum_excl = jnp.where(row == c - 1, 0.0, pltpu.roll(cum, c - 1, axis=0))
                        tot = cum[0:1, :]
                    else:
                        cum_excl = jnp.where(row == 0, 0.0, pltpu.roll(cum, 1, axis=0))
                        tot = cum[c - 1:c, :]
                    kk = kk_ref[n, rows, sl].astype(F32)
                    return dict(r=r_ref[n, rows, sl].astype(F32), v=v_ref[n, rows, sl].astype(F32), kk=kk,
                                kd=kd_ref[n, rows, sl].astype(F32), b=kk * ic_ref[n, rows, sl].astype(F32),
                                dinc=jnp.exp(cum), dexc=jnp.exp(cum_excl), dinv=jnp.exp(-cum),
                                dend=jnp.exp(tot - cum), dtot=jnp.exp(tot))

                for g in range(r_ref.shape[-1] // GROUP_LANES):
                    sl = slice(g * GROUP_LANES, (g + 1) * GROUP_LANES)
                    steps[k].append(dict(load=functools.partial(load, sl=sl), reverse=reverse, n=n, d=d, g=g,
                                         sl=sl, rows=rows))

    _scan_prepare([w for units in steps for w in units], with_y)
    state = {}
    for units in steps:
        for w in units:
            key = (w["n"], w["d"], w["g"])
            w["s"] = state[key] if key in state else s_sc[key]
        for w, (s_new, y) in zip(units, _scan_apply(units, with_y)):
            state[(w["n"], w["d"], w["g"])] = s_new
            if with_y:
                y_refs[w["d"]][w["n"], w["rows"], w["sl"]] = y
    for key, s_new in state.items():
        s_sc[key] = s_new

    @pl.when(ci == pl.num_programs(1) - 1)
    def _():
        sfin_ref[...] = s_sc[...]


def _scan(q, s0, with_y, n_sub):
    bsz, t, rw = q["r"].shape
    c = SCAN_CHUNK * n_sub
    assert t % c == 0
    nb = SCAN_BATCH if bsz % SCAN_BATCH == 0 else 1
    nc = t // c
    ng = rw // GROUP_LANES
    fwd = pl.BlockSpec((nb, c, rw), lambda b, i: (b, i, 0))
    bwd = pl.BlockSpec((nb, c, rw), lambda b, i: (b, nc - 1 - i, 0))
    s_spec = pl.BlockSpec((nb, 2, ng, RWKV_HEAD, GROUP_LANES), lambda b, i: (b, 0, 0, 0, 0))
    args = [q["r"], q["v"], q["kk"], q["cum0"], q["kd0"], q["ic0"],
            q["r"], q["v"], q["kk"], q["cum1"], q["kd1"], q["ic1"], s0]
    in_specs = [fwd] * 6 + [bwd] * 6 + [s_spec]
    y_shape = jax.ShapeDtypeStruct((bsz, t, rw), F32)
    out_specs = ([fwd, bwd] if with_y else []) + [s_spec]
    out_shape = ([y_shape, y_shape] if with_y else []) + [jax.ShapeDtypeStruct(s0.shape, F32)]
    outs = pl.pallas_call(
        functools.partial(_scan_kernel, with_y=with_y),
        grid=(bsz // nb, nc),
        in_specs=in_specs,
        out_specs=out_specs,
        out_shape=out_shape,
        scratch_shapes=[pltpu.VMEM((nb, 2, ng, RWKV_HEAD, GROUP_LANES), F32)],
        compiler_params=_cparams(("parallel", "arbitrary")),
        name="scan_y" if with_y else "scan_state",
    )(*args)
    return outs


def _mixout_kernel(yf_ref, yb_ref, r_ref, v_ref, g_ref, kd0_ref, kd1_ref, gm_ref, x_ref, mod_ref,
                   rk_ref, gng_ref, gnb_ref, glg_ref, glb_ref, ws_ref, bs_ref, wout_ref, lng_ref, lnb_ref,
                   bd_ref, o_ref, *, slot):
    bd = bd_ref[...]
    rw = yf_ref.shape[-1]
    inv_n = 1.0 / RWKV_HEAD
    inv_g = 1.0 / GMLP_GROUP
    f32 = lambda ref: ref[...].astype(F32)
    y = yf_ref[...] + yb_ref[...]
    vv = _gelu_tanh(gm_ref[:, rw:].astype(F32))
    mu = _head_sum(y, bd) * inv_n
    mu_v = _head_sum(vv, bd) * inv_g
    bonus_in = f32(r_ref) * (f32(kd0_ref) + f32(kd1_ref)) * rk_ref[...]
    dy = y - mu
    var = _head_sum(dy * dy, bd) * inv_n
    dv = vv - mu_v
    var_v = _head_sum(dv * dv, bd) * inv_g
    bonus = _head_sum(bonus_in, bd) * f32(v_ref)
    yn = dy * jax.lax.rsqrt(var + GN_EPS) * gng_ref[...] + gnb_ref[...]
    out_r = ((yn + bonus) * f32(g_ref)).astype(BF16)
    vn = (dv * jax.lax.rsqrt(var_v + LN_EPS) * glg_ref[...] + glb_ref[...]).astype(BF16)
    o_r = _dot(out_r, wout_ref[0:rw, :])
    u = _gelu_tanh(gm_ref[:, :rw].astype(F32))
    tm = vn.shape[0]
    lane = jax.lax.broadcasted_iota(jnp.int32, (CHUNK, 2 * GMLP_GROUP), 1)
    chunks = []
    for n in range(tm // CHUNK):
        pairs = []
        for gp in range(rw // (2 * GMLP_GROUP)):
            v2 = vn[n * CHUNK:(n + 1) * CHUNK, gp * 2 * GMLP_GROUP:(gp + 1) * 2 * GMLP_GROUP]
            m0 = _dot(ws_ref[2 * gp], v2)
            m1 = _dot(ws_ref[2 * gp + 1], v2)
            pairs.append(jnp.where(lane < GMLP_GROUP, m0, m1))
        chunks.append(jnp.concatenate(pairs, axis=1) + bs_ref[...])
    mixed = jnp.concatenate(chunks, axis=0)
    out_g = (u * mixed).astype(BF16)

    o = o_r + _dot(out_g, wout_ref[rw:, :])
    gate = mod_ref[0, 3 * slot + 2:3 * slot + 3, :]
    z = ALPHA * x_ref[...] + gate * o
    o_ref[...] = _layer_norm(z, lng_ref[...], lnb_ref[...])


def _mixout(yf, yb, q, p_gm, x2, mods, mod_row, params, bd, slot, tm):
    ntok, d = x2.shape
    rw = yf.shape[-1]
    tok = lambda w: pl.BlockSpec((tm, w), lambda i: (i, 0))
    const = lambda a: pl.BlockSpec(a.shape, lambda i: (0,) * a.ndim, pipeline_mode=pl.Buffered(1))
    plist = [params[k] for k in ("r_k", "gn_g", "gn_b", "gm_ln_g", "gm_ln_b", "gm_ws", "gm_bs_full", "w_out",
                                 "ln_g", "ln_b")] + [bd]
    return pl.pallas_call(
        functools.partial(_mixout_kernel, slot=slot),
        grid=(ntok // tm,),
        in_specs=[tok(rw)] * 7 + [tok(p_gm.shape[1]), tok(d),
                                  pl.BlockSpec((1, N_MOD, d), lambda i: (mod_row(i), 0, 0))]
                 + [const(a) for a in plist],
        out_specs=tok(d),
        out_shape=jax.ShapeDtypeStruct((ntok, d), F32),
        compiler_params=_cparams(("parallel",)),
        name="mixout",
    )(yf, yb, q["r"], q["v"], q["g"], q["kd0"], q["kd1"], p_gm, x2, mods, *plist)


def kernel(x, c, ctx, c_ctx, w_ada, b_ada, ln_g, ln_b, ffn_a_wi, ffn_a_wo, ffn_b_wi, ffn_b_wo, w_in, mu_shift,
           w0, w_up, a0, a_up, g_up, k_k, k_a, r_k, gn_g, gn_b, gm_ln_g, gm_ln_b, gm_ws, gm_bs, w_out):
    bsz, seq, d = x.shape
    ctx_len = ctx.shape[1]
    assert w_ada.shape[0] == DEPTH
    rw = w0.shape[-1]
    rwkv_in = mu_shift.shape[-1]
    w_lora, a_lora = w_up.shape[2], a_up.shape[2]
    assert w_lora + a_lora == 128 and g_up.shape[1] == 128 and rwkv_in == 3 * rw + 256
    assert seq % GRID_W == 0 and seq % SCAN_CHUNK == 0 and ctx_len % SCAN_CHUNK == 0 and seq % CHUNK == 0
    i = 0

    n_rows = -(-(bsz + 1) // 8) * 8
    cc = jnp.zeros((n_rows, d), F32).at[:bsz].set(c).at[bsz].set(c_ctx)
    mods = _ada(cc, w_ada[i], b_ada[i]).reshape(n_rows, N_MOD, d)

    x2 = x.reshape(bsz * seq, d)
    c2 = ctx.reshape(bsz * ctx_len, d)
    tm_fx = _pick(seq, FFN_TM)
    tm_fc = _pick(bsz * ctx_len, FFN_TM)
    row_x = lambda tm: (lambda t: t // (seq // tm))
    row_c = lambda t: bsz

    bf = lambda a: a.astype(BF16)
    wi_a, wo_a = _ffn_weights(ffn_a_wi[i], ffn_a_wo[i])
    x1 = _ffn(x2, mods, row_x(tm_fx), wi_a, wo_a, ln_g[i, 0], ln_b[i, 0], 0, tm_fx)
    c1 = _ffn(c2, mods, row_c, wi_a, wo_a, ln_g[i, 0], ln_b[i, 0], 0, tm_fc)

    w_rw, w_gm = bf(w_in[i, :, :rwkv_in]), bf(w_in[i, :, rwkv_in:])
    p_rw_x = _inproj(x1, mods, row_x(tm_fx), w_rw, 1, tm_fx, rwkv_in // 2)
    p_rw_c = _inproj(c1, mods, row_c, w_rw, 1, tm_fc, rwkv_in // 2)

    zpad = lambda a, lo, hi: jnp.pad(a, ((0, 0), (lo, hi), (0, 0)))
    lanes = jnp.arange(GROUP_LANES) // RWKV_HEAD
    bd = (lanes[:, None] == lanes[None, :]).astype(BF16)
    prm = dict(mu=mu_shift[i].reshape(1, rwkv_in), w0=w0[i], w_up=bf(zpad(w_up[i], 0, a_lora)), a0=a0[i],
               a_up=bf(zpad(a_up[i], w_lora, 0)), g_up=bf(g_up[i]), k_k=k_k[i].reshape(1, rw),
               k_a=k_a[i].reshape(1, rw))
    tm_p = _pick(seq, 256)
    q_x = _prep(p_rw_x, prm, bd, seq, tm_p, gm=(x1, mods, row_x(tm_p), w_gm, 1))
    q_c = _prep(p_rw_c, prm, bd, ctx_len, ctx_len)
    p_gm_x = q_x.pop("p_gm")
    q_x3 = {k: a.reshape(bsz, seq, rw) for k, a in q_x.items()}
    q_c3 = {k: a.reshape(bsz, ctx_len, rw) for k, a in q_c.items()}

    ng = rw // GROUP_LANES
    s_zero = jnp.zeros((bsz, 2, ng, RWKV_HEAD, GROUP_LANES), F32)
    (s_ctx,) = _scan(q_c3, s_zero, with_y=False, n_sub=1)
    yf, yb, _ = _scan(q_x3, s_ctx, with_y=True, n_sub=SCAN_SUB)

    out_prm = dict(r_k=r_k[i].reshape(1, rw), gn_g=gn_g[i].reshape(1, rw), gn_b=gn_b[i].reshape(1, rw),
                   gm_ln_g=gm_ln_g[i].reshape(1, rw), gm_ln_b=gm_ln_b[i].reshape(1, rw), gm_ws=bf(gm_ws[i]),
                   gm_bs_full=jnp.repeat(gm_bs[i].T, GMLP_GROUP, axis=1), w_out=bf(w_out[i]),
                   ln_g=ln_g[i, 1].reshape(1, d), ln_b=ln_b[i, 1].reshape(1, d))
    tm_o = _pick(seq, 512)
    x2b = _mixout(yf.reshape(bsz * seq, rw), yb.reshape(bsz * seq, rw), q_x, p_gm_x, x1, mods, row_x(tm_o),
                  out_prm, bd, 1, tm_o)

    wi_b, wo_b = _ffn_weights(ffn_b_wi[i], ffn_b_wo[i])
    x3 = _ffn(x2b, mods, row_x(tm_fx), wi_b, wo_b, ln_g[i, 2], ln_b[i, 2], 2, tm_fx)
    return x3.reshape(bsz, seq, d)
```

```python
import functools
import math

import jax
import jax.numpy as jnp
from jax.experimental import pallas as pl
from jax.experimental.pallas import tpu as pltpu

F32 = jnp.float32
BF16 = jnp.bfloat16

GRID_W = 64
RWKV_HEAD = 64
CHUNK = 128
GMLP_GROUP = 64
N_MOD = 9
LN_EPS = 1e-5
GN_EPS = 64e-5
DEPTH = 1
ALPHA = (2.0 * DEPTH) ** 0.25

SCAN_CHUNK = 64
HEADS_PER_GROUP = 4
GROUP_LANES = HEADS_PER_GROUP * RWKV_HEAD
SCAN_BATCH = 2
SCAN_SUB = 2
VMEM_LIMIT = 56 * 1024 * 1024


def _cparams(sem, vmem_limit=VMEM_LIMIT):
    return pltpu.CompilerParams(dimension_semantics=sem, vmem_limit_bytes=vmem_limit)


def _pick(n, pref):
    t = min(n, pref)
    while n % t:
        t -= 64
    return t


def _sigmoid(z):
    return 1.0 / (1.0 + jnp.exp(-z))


def _silu(z):
    return z * _sigmoid(z)


def _gelu_tanh(z):
    return 0.5 * z * (1.0 + jnp.tanh(math.sqrt(2.0 / math.pi) * (z + 0.044715 * (z * z * z))))


def _layer_norm(z, g, b):
    mu = jnp.mean(z, axis=-1, keepdims=True)
    d = z - mu
    var = jnp.mean(d * d, axis=-1, keepdims=True)
    return d * jax.lax.rsqrt(var + LN_EPS) * g + b


def _dot(a, b):
    return jnp.dot(a, b, preferred_element_type=F32)


def _dot_nt(a, b):
    return jax.lax.dot_general(a, b, (((1,), (1,)), ((), ())), preferred_element_type=F32)


def _group_sum(z, bd):
    return _dot(z.astype(BF16), bd)


def _head_sum(z, bd):
    w = z.shape[-1]
    return jnp.concatenate([_group_sum(z[:, s:s + GROUP_LANES], bd) for s in range(0, w, GROUP_LANES)], axis=1)


def _ada_kernel(c_ref, w_ref, b_ref, o_ref):
    a = _silu(c_ref[...]).astype(BF16)
    o_ref[...] = _dot(a, w_ref[...].astype(BF16)) + b_ref[...]


def _ada(cc, w_ada, b_ada):
    m, d = cc.shape
    n = w_ada.shape[1]
    tn = 1024
    return pl.pallas_call(
        _ada_kernel,
        grid=(n // tn,),
        in_specs=[pl.BlockSpec((m, d), lambda j: (0, 0)),
                  pl.BlockSpec((d, tn), lambda j: (0, j)),
                  pl.BlockSpec((1, tn), lambda j: (0, j))],
        out_specs=pl.BlockSpec((m, tn), lambda j: (0, j)),
        out_shape=jax.ShapeDtypeStruct((m, n), F32),
        compiler_params=_cparams(("arbitrary",)),
        name="ada",
    )(cc, w_ada, b_ada.reshape(1, n))


def _ffn_kernel(x_ref, mod_ref, wig_ref, wiu_ref, wo_ref, g_ref, b_ref, o_ref, h_sc, *, slot):
    f = pl.program_id(1)

    @pl.when(f == 0)
    def _():
        shift = mod_ref[0, 3 * slot:3 * slot + 1, :]
        scale = mod_ref[0, 3 * slot + 1:3 * slot + 2, :]
        h_sc[...] = (x_ref[...] * (1.0 + scale) + shift).astype(BF16)
        o_ref[...] = jnp.zeros_like(o_ref)

    half = wo_ref.shape[0] // 2
    cols = [slice(j * half, (j + 1) * half) for j in range(2)]
    n_rows = min(FFN_ROWS, h_sc.shape[0])
    for r0 in range(0, h_sc.shape[0], n_rows):
        rows = slice(r0, r0 + n_rows)
        h = h_sc[rows, :]
        gate_up = [(_dot(h, wig_ref[0, :, c]), _dot(h, wiu_ref[0, :, c])) for c in cols]
        out = None
        for c, (gate, up) in zip(cols, gate_up):
            part = _dot((_silu(gate) * up).astype(BF16), wo_ref[c, :])
            out = part if out is None else out + part
        o_ref[rows, :] += out

    @pl.when(f == pl.num_programs(1) - 1)
    def _():
        gmod = mod_ref[0, 3 * slot + 2:3 * slot + 3, :]
        z = ALPHA * x_ref[...] + 0.5 * gmod * o_ref[...]
        o_ref[...] = _layer_norm(z, g_ref[...], b_ref[...])


FFN_TF = 512
FFN_TM = 1024
FFN_ROWS = 512
FFN_VMEM_LIMIT = 62 * 1024 * 1024


def _cast_tiles_kernel(w_ref, o_ref):
    o_ref[0] = w_ref[...].astype(BF16)


def _cast_col_tiles(w, tn):
    d, n = w.shape
    return pl.pallas_call(
        _cast_tiles_kernel,
        grid=(n // tn,),
        in_specs=[pl.BlockSpec((d, tn), lambda j: (0, j))],
        out_specs=pl.BlockSpec((1, d, tn), lambda j: (j, 0, 0)),
        out_shape=jax.ShapeDtypeStruct((n // tn, d, tn), BF16),
        compiler_params=_cparams(("parallel",)),
        name="cast_tiles",
    )(w)


def _ffn_weights(wi, wo):
    return _cast_col_tiles(wi, FFN_TF), wo.astype(BF16)


def _ffn(x2, mods, mod_row, wi, wo, ln_g, ln_b, slot, tm):
    ntok, d = x2.shape
    dff = wo.shape[0]
    tf = FFN_TF
    nf = dff // tf
    return pl.pallas_call(
        functools.partial(_ffn_kernel, slot=slot),
        grid=(ntok // tm, nf),
        in_specs=[pl.BlockSpec((tm, d), lambda i, f: (i, 0)),
                  pl.BlockSpec((1, N_MOD, d), lambda i, f: (mod_row(i), 0, 0)),
                  pl.BlockSpec((1, d, tf), lambda i, f: (f, 0, 0)),
                  pl.BlockSpec((1, d, tf), lambda i, f: (nf + f, 0, 0)),
                  pl.BlockSpec((tf, d), lambda i, f: (f, 0)),
                  pl.BlockSpec((1, d), lambda i, f: (0, 0)),
                  pl.BlockSpec((1, d), lambda i, f: (0, 0))],
        out_specs=pl.BlockSpec((tm, d), lambda i, f: (i, 0)),
        out_shape=jax.ShapeDtypeStruct((ntok, d), F32),
        scratch_shapes=[pltpu.VMEM((tm, d), BF16)],
        compiler_params=_cparams(("parallel", "arbitrary"), FFN_VMEM_LIMIT),
        name=f"ffn{slot}",
    )(x2, mods, wi, wi, wo, ln_g.reshape(1, d), ln_b.reshape(1, d))


def _inproj_kernel(x_ref, mod_ref, w_ref, o_ref, *, slot):
    shift = mod_ref[0, 3 * slot:3 * slot + 1, :]
    scale = mod_ref[0, 3 * slot + 1:3 * slot + 2, :]
    h = (x_ref[...] * (1.0 + scale) + shift).astype(BF16)
    o_ref[...] = _dot(h, w_ref[...])


def _inproj(x2, mods, mod_row, w, slot, tm, tn):
    ntok, d = x2.shape
    n = w.shape[1]
    return pl.pallas_call(
        functools.partial(_inproj_kernel, slot=slot),
        grid=(n // tn, ntok // tm),
        in_specs=[pl.BlockSpec((tm, d), lambda j, i: (i, 0)),
                  pl.BlockSpec((1, N_MOD, d), lambda j, i: (mod_row(i), 0, 0)),
                  pl.BlockSpec((d, tn), lambda j, i: (0, j))],
        out_specs=pl.BlockSpec((tm, tn), lambda j, i: (i, j)),
        out_shape=jax.ShapeDtypeStruct((ntok, n), F32),
        compiler_params=_cparams(("parallel", "parallel")),
        name="inproj",
    )(x2, mods, w)


_PREP_NAMES = ("r", "v", "g", "kk", "kd0", "kd1", "ic0", "ic1", "cum0", "cum1")
_PREP_DTYPES = (BF16,) * 8 + (F32,) * 2


def _prep_kernel(*refs, grid_mode, tiles_per_seq, rw, slot):
    if grid_mode:
        p_ref, up_ref, dn_ref, x_ref, mod_ref, wgm_ref = refs[:6]
        refs = refs[6:]
    else:
        p_ref = refs[0]
        refs = refs[1:]
    (mu_ref, w0_ref, wup_ref, a0_ref, aup_ref, gup_ref, kk_ref, ka_ref, bd_ref, tri0_ref, tri1_ref,
     r_o, v_o, g_o, kk_o, kd0_o, kd1_o, ic0_o, ic1_o, cum0_o, cum1_o) = refs[:21]
    tm = p_ref.shape[0]
    n_gm = 4

    def gm_chunk(j):
        if grid_mode:
            pgm_o = refs[21]
            wn = pgm_o.shape[1] // n_gm
            pgm_o[:, j * wn:(j + 1) * wn] = _dot(h_gm, wgm_ref[:, j * wn:(j + 1) * wn]).astype(BF16)

    def mixed(lo, hi):
        p = p_ref[:, lo:hi]
        width = hi - lo
        row = jax.lax.broadcasted_iota(jnp.int32, (tm, width), 0)
        lane = jax.lax.broadcasted_iota(jnp.int32, (tm, width), 1)
        prev = pltpu.roll(p, 1, axis=0)
        nxt = pltpu.roll(p, tm - 1, axis=0)
        if grid_mode:
            i = pl.program_id(0) % tiles_per_seq
            col = row % GRID_W
            left = jnp.where(col > 0, prev, 0.0)
            right = jnp.where(col < GRID_W - 1, nxt, 0.0)
            up_halo = jnp.where(i > 0, up_ref[:, lo:hi], 0.0)
            dn_halo = jnp.where(i < tiles_per_seq - 1, dn_ref[:, lo:hi], 0.0)
            up = jnp.concatenate([up_halo, p[:tm - GRID_W]], axis=0)
            down = jnp.concatenate([p[GRID_W:], dn_halo], axis=0)
            c4 = lane % 4
            shifted = jnp.where(c4 == 0, left, jnp.where(c4 == 1, right, jnp.where(c4 == 2, up, down)))
        else:
            prev = jnp.where(row > 0, prev, 0.0)
            nxt = jnp.where(row < tm - 1, nxt, 0.0)
            shifted = jnp.where(lane % 2 == 0, prev, nxt)
        return p + (shifted - p) * mu_ref[:, lo:hi]

    if grid_mode:
        shift = mod_ref[0, 3 * slot:3 * slot + 1, :]
        scale = mod_ref[0, 3 * slot + 1:3 * slot + 2, :]
        h_gm = (x_ref[...] * (1.0 + scale) + shift).astype(BF16)
    wa_lo = mixed(3 * rw, 3 * rw + 128)
    g_lo = mixed(3 * rw + 128, 3 * rw + 256)
    tanh_wa = jnp.tanh(wa_lo).astype(BF16)
    wa_bf = wa_lo.astype(BF16)
    sig_g = _sigmoid(g_lo).astype(BF16)
    gm_chunk(0)
    k = mixed(rw, 2 * rw)
    kk = k * kk_ref[...]
    kk_sq = kk * kk
    g_lin = _dot(sig_g, gup_ref[...])
    z_w = [_dot(tanh_wa, wup_ref[d]) for d in range(2)]
    z_a = [_dot(wa_bf, aup_ref[d]) for d in range(2)]
    gm_chunk(1)
    ss = _head_sum(kk_sq, bd_ref[...])
    gm_chunk(2)
    g_o[...] = g_lin.astype(BF16)
    r_o[...] = mixed(0, rw).astype(BF16)
    v_o[...] = mixed(2 * rw, 3 * rw).astype(BF16)
    kk_o[...] = (kk / jnp.maximum(jnp.sqrt(ss), 1e-12)).astype(BF16)
    for d, (tri_ref, cum_o, kd_o, ic_o) in enumerate(((tri0_ref, cum0_o, kd0_o, ic0_o),
                                                      (tri1_ref, cum1_o, kd1_o, ic1_o))):
        lw = -math.exp(-0.5) * _sigmoid(w0_ref[d:d + 1, :] + z_w[d])
        hi = lw.astype(BF16)
        rem = lw - hi.astype(F32)
        mid = rem.astype(BF16)
        lo = (rem - mid.astype(F32)).astype(BF16)
        tri = tri_ref[...]
        cum_o[...] = _dot(tri, hi) + _dot(tri, mid) + _dot(tri, lo)
        iclr = _sigmoid(a0_ref[d:d + 1, :] + z_a[d])
        ic_o[...] = iclr.astype(BF16)
        kd_o[...] = (k * (1.0 + (iclr - 1.0) * ka_ref[...])).astype(BF16)
    gm_chunk(3)


def _prep(p_rw, params, bd, tokens_per_seq, tm, gm=None):
    ntok, width = p_rw.shape
    rw = params["w0"].shape[1]
    grid_mode = gm is not None
    tiles_per_seq = tokens_per_seq // tm
    hb = tm // GRID_W
    nhalo = ntok // GRID_W
    const = lambda a: pl.BlockSpec(a.shape, lambda i: (0,) * a.ndim, pipeline_mode=pl.Buffered(1))
    tok = lambda w: pl.BlockSpec((tm, w), lambda i: (i, 0))
    in_specs = [tok(width)]
    args = [p_rw]
    names, dtypes, widths = list(_PREP_NAMES), list(_PREP_DTYPES), [rw] * 10
    slot = 0
    if grid_mode:
        x2, mods, mod_row, w_gm, slot = gm
        d = x2.shape[1]
        in_specs += [pl.BlockSpec((GRID_W, width), lambda i: (jnp.maximum(i * hb - 1, 0), 0)),
                     pl.BlockSpec((GRID_W, width), lambda i: (jnp.minimum((i + 1) * hb, nhalo - 1), 0)),
                     tok(d), pl.BlockSpec((1, N_MOD, d), lambda i: (mod_row(i), 0, 0)), const(w_gm)]
        args += [p_rw, p_rw, x2, mods, w_gm]
        names.append("p_gm")
        dtypes.append(BF16)
        widths.append(w_gm.shape[1])
    ti = jnp.arange(tm)[:, None]
    tj = jnp.arange(tm)[None, :]
    same = (ti // SCAN_CHUNK) == (tj // SCAN_CHUNK)
    tri = [(same & (tj <= ti)).astype(BF16), (same & (tj >= ti)).astype(BF16)]
    plist = [params[k] for k in ("mu", "w0", "w_up", "a0", "a_up", "g_up", "k_k", "k_a")] + [bd] + tri
    in_specs += [const(a) for a in plist]
    outs = pl.pallas_call(
        functools.partial(_prep_kernel, grid_mode=grid_mode, tiles_per_seq=tiles_per_seq, rw=rw, slot=slot),
        grid=(ntok // tm,),
        in_specs=in_specs,
        out_specs=[tok(w) for w in widths],
        out_shape=[jax.ShapeDtypeStruct((ntok, w), dt) for w, dt in zip(widths, dtypes)],
        compiler_params=_cparams(("parallel",)),
        name="prep_grid" if grid_mode else "prep_seq",
    )(*args, *plist)
    return dict(zip(names, outs))


def _stackmask(z, lane_head):
    return jnp.concatenate([jnp.where(lane_head == h, z, jnp.zeros_like(z)) for h in range(HEADS_PER_GROUP)],
                           axis=0)


def _scan_prepare(units, with_y):
    c, l = SCAN_CHUNK, GROUP_LANES
    gc = HEADS_PER_GROUP * c
    n_lvl = int(math.log2(c))
    t = jax.lax.broadcasted_iota(jnp.int32, (c, gc), 0)
    tj = jax.lax.broadcasted_iota(jnp.int32, (c, gc), 1) % c
    lane_head = jax.lax.broadcasted_iota(jnp.int32, (c, l), 1) // RWKV_HEAD
    bd = functools.partial(_stackmask, lane_head=lane_head)
    eye = jnp.where(tj == t, 1.0, 0.0)

    for w in units:
        w.update(w["load"]())
        at = (-w["kk"] * w["dexc"]).astype(BF16)
        rt = (w["r"] * w["dinc"]).astype(BF16)
        bt = (w["b"] * w["dinv"]).astype(BF16)
        kt = (w["kd"] * w["dinv"]).astype(BF16)
        w["lhs"] = jnp.concatenate([at, rt], axis=0) if with_y else at
        w["a_all"] = _dot_nt(w["lhs"], jnp.concatenate([bd(bt), bd(kt)], axis=0))
        w["bk"] = jnp.concatenate([w["b"] * w["dend"], w["kd"] * w["dend"]], axis=0).astype(BF16)
    for w in units:
        strict = (tj > t) if w["reverse"] else (tj < t)
        a_ab = jnp.where(strict, w["a_all"][:c, :gc], 0.0)
        a_v = jnp.where(strict, w["a_all"][:c, gc:], 0.0)
        if with_y:
            incl = (tj >= t) if w["reverse"] else (tj <= t)
            a_v = jnp.concatenate([a_v, jnp.where(incl, w["a_all"][c:, gc:], 0.0)], axis=0)
            w["a_rb"] = jnp.where(incl, w["a_all"][c:, :gc], 0.0).astype(BF16)
        w["from_v"] = _dot(a_v.astype(BF16), bd(w["v"].astype(BF16)))
        w["p"] = eye + a_ab
        w["apow"] = a_ab.astype(BF16)
    for w in units:
        w["apow"] = _dot(w["apow"], bd(w["apow"])).astype(BF16)
    for lvl in range(1, n_lvl):
        last = lvl == n_lvl - 1
        for w in units:
            p_bf = w["p"].astype(BF16)
            if last:
                w["p"] = (w["p"] + _dot(p_bf, bd(w["apow"]))).astype(BF16)
            else:
                both = _dot(jnp.concatenate([p_bf, w["apow"]], axis=0), bd(w["apow"]))
                w["p"] = w["p"] + both[:c]
                w["apow"] = both[c:].astype(BF16)


def _scan_apply(units, with_y):
    c, l = SCAN_CHUNK, GROUP_LANES
    lane_head = jax.lax.broadcasted_iota(jnp.int32, (c, l), 1) // RWKV_HEAD
    bd = functools.partial(_stackmask, lane_head=lane_head)
    rh = jax.lax.broadcasted_iota(jnp.int32, (l, l), 0) // RWKV_HEAD
    ch = jax.lax.broadcasted_iota(jnp.int32, (l, l), 1) // RWKV_HEAD
    for w in units:
        w["from_s"] = _dot_nt(w["lhs"], w["s"].astype(BF16))
    for w in units:
        u = w["from_s"][:c] + w["from_v"][:c]
        w["sa"] = _dot(w["p"], bd(u.astype(BF16)))
    outs = []
    for w in units:
        y = None
        if with_y:
            y = w["from_s"][c:] + w["from_v"][c:] + _dot(w["a_rb"], bd(w["sa"].astype(BF16)))
        sav_t = jnp.concatenate([w["sa"], w["v"]], axis=0).T.astype(BF16)
        upd = _dot(sav_t, w["bk"])
        outs.append((w["s"] * w["dtot"] + jnp.where(rh == ch, upd, 0.0), y))
    return outs


def _scan_kernel(*refs, with_y):
    ins = refs[:13]
    s0_ref = ins[12]
    if with_y:
        yf_ref, yb_ref, sfin_ref, s_sc = refs[13:]
        y_refs = (yf_ref, yb_ref)
    else:
        sfin_ref, s_sc = refs[13:]
        y_refs = (None, None)
    ci = pl.program_id(1)

    @pl.when(ci == 0)
    def _():
        s_sc[...] = s0_ref[...]

    c = SCAN_CHUNK
    n_sub = ins[0].shape[1] // c
    steps = [[] for _ in range(n_sub)]
    for n in range(s_sc.shape[0]):
        for d in range(2):
            r_ref, v_ref, kk_ref, cum_ref, kd_ref, ic_ref = ins[6 * d:6 * d + 6]
            reverse = d == 1
            for k in range(n_sub):
                rows = pl.ds((n_sub - 1 - k if reverse else k) * c, c)

                def load(n=n, reverse=reverse, r_ref=r_ref, v_ref=v_ref, kk_ref=kk_ref, cum_ref=cum_ref,
                         kd_ref=kd_ref, ic_ref=ic_ref, rows=rows, sl=None):
                    cum = cum_ref[n, rows, sl]
                    row = jax.lax.broadcasted_iota(jnp.int32, cum.shape, 0)
                    if reverse:
                        cum_excl = jnp.where(row == c - 1, 0.0, pltpu.roll(cum, c - 1, axis=0))
                        tot = cum[0:1, :]
                    else:
                        cum_excl = jnp.where(row == 0, 0.0, pltpu.roll(cum, 1, axis=0))
                        tot = cum[c - 1:c, :]
                    kk = kk_ref[n, rows, sl].astype(F32)
                    return dict(r=r_ref[n, rows, sl].astype(F32), v=v_ref[n, rows, sl].astype(F32), kk=kk,
                                kd=kd_ref[n, rows, sl].astype(F32), b=kk * ic_ref[n, rows, sl].astype(F32),
                                dinc=jnp.exp(cum), dexc=jnp.exp(cum_excl), dinv=jnp.exp(-cum),
                                dend=jnp.exp(tot - cum), dtot=jnp.exp(tot))

                for g in range(r_ref.shape[-1] // GROUP_LANES):
                    sl = slice(g * GROUP_LANES, (g + 1) * GROUP_LANES)
                    steps[k].append(dict(load=functools.partial(load, sl=sl), reverse=reverse, n=n, d=d, g=g,
                                         sl=sl, rows=rows))

    _scan_prepare([w for units in steps for w in units], with_y)
    state = {}
    for units in steps:
        for w in units:
            key = (w["n"], w["d"], w["g"])
            w["s"] = state[key] if key in state else s_sc[key]
        for w, (s_new, y) in zip(units, _scan_apply(units, with_y)):
            state[(w["n"], w["d"], w["g"])] = s_new
            if with_y:
                y_refs[w["d"]][w["n"], w["rows"], w["sl"]] = y
    for key, s_new in state.items():
        s_sc[key] = s_new

    @pl.when(ci == pl.num_programs(1) - 1)
    def _():
        sfin_ref[...] = s_sc[...]


def _scan(q, s0, with_y, n_sub):
    bsz, t, rw = q["r"].shape
    c = SCAN_CHUNK * n_sub
    assert t % c == 0
    nb = SCAN_BATCH if bsz % SCAN_BATCH == 0 else 1
    nc = t // c
    ng = rw // GROUP_LANES
    fwd = pl.BlockSpec((nb, c, rw), lambda b, i: (b, i, 0))
    bwd = pl.BlockSpec((nb, c, rw), lambda b, i: (b, nc - 1 - i, 0))
    s_spec = pl.BlockSpec((nb, 2, ng, GROUP_LANES, GROUP_LANES), lambda b, i: (b, 0, 0, 0, 0))
    args = [q["r"], q["v"], q["kk"], q["cum0"], q["kd0"], q["ic0"],
            q["r"], q["v"], q["kk"], q["cum1"], q["kd1"], q["ic1"], s0]
    in_specs = [fwd] * 6 + [bwd] * 6 + [s_spec]
    y_shape = jax.ShapeDtypeStruct((bsz, t, rw), F32)
    out_specs = ([fwd, bwd] if with_y else []) + [s_spec]
    out_shape = ([y_shape, y_shape] if with_y else []) + [jax.ShapeDtypeStruct(s0.shape, F32)]
    outs = pl.pallas_call(
        functools.partial(_scan_kernel, with_y=with_y),
        grid=(bsz // nb, nc),
        in_specs=in_specs,
        out_specs=out_specs,
        out_shape=out_shape,
        scratch_shapes=[pltpu.VMEM((nb, 2, ng, GROUP_LANES, GROUP_LANES), F32)],
        compiler_params=_cparams(("parallel", "arbitrary")),
        name="scan_y" if with_y else "scan_state",
    )(*args)
    return outs


def _mixout_kernel(yf_ref, yb_ref, r_ref, v_ref, g_ref, kd0_ref, kd1_ref, gm_ref, x_ref, mod_ref,
                   rk_ref, gng_ref, gnb_ref, glg_ref, glb_ref, ws_ref, bs_ref, wout_ref, lng_ref, lnb_ref,
                   bd_ref, o_ref, *, slot):
    bd = bd_ref[...]
    rw = yf_ref.shape[-1]
    inv_n = 1.0 / RWKV_HEAD
    inv_g = 1.0 / GMLP_GROUP
    f32 = lambda ref: ref[...].astype(F32)
    y = yf_ref[...] + yb_ref[...]
    vv = _gelu_tanh(gm_ref[:, rw:].astype(F32))
    mu = _head_sum(y, bd) * inv_n
    mu_v = _head_sum(vv, bd) * inv_g
    bonus_in = f32(r_ref) * (f32(kd0_ref) + f32(kd1_ref)) * rk_ref[...]
    dy = y - mu
    var = _head_sum(dy * dy, bd) * inv_n
    dv = vv - mu_v
    var_v = _head_sum(dv * dv, bd) * inv_g
    bonus = _head_sum(bonus_in, bd) * f32(v_ref)
    yn = dy * jax.lax.rsqrt(var + GN_EPS) * gng_ref[...] + gnb_ref[...]
    out_r = ((yn + bonus) * f32(g_ref)).astype(BF16)
    vn = (dv * jax.lax.rsqrt(var_v + LN_EPS) * glg_ref[...] + glb_ref[...]).astype(BF16)
    o_r = _dot(out_r, wout_ref[0:rw, :])
    u = _gelu_tanh(gm_ref[:, :rw].astype(F32))
    tm = vn.shape[0]
    lane = jax.lax.broadcasted_iota(jnp.int32, (CHUNK, 2 * GMLP_GROUP), 1)
    chunks = []
    for n in range(tm // CHUNK):
        pairs = []
        for gp in range(rw // (2 * GMLP_GROUP)):
            v2 = vn[n * CHUNK:(n + 1) * CHUNK, gp * 2 * GMLP_GROUP:(gp + 1) * 2 * GMLP_GROUP]
            m0 = _dot(ws_ref[2 * gp], v2)
            m1 = _dot(ws_ref[2 * gp + 1], v2)
            pairs.append(jnp.where(lane < GMLP_GROUP, m0, m1))
        chunks.append(jnp.concatenate(pairs, axis=1) + bs_ref[...])
    mixed = jnp.concatenate(chunks, axis=0)
    out_g = (u * mixed).astype(BF16)

    o = o_r + _dot(out_g, wout_ref[rw:, :])
    gate = mod_ref[0, 3 * slot + 2:3 * slot + 3, :]
    z = ALPHA * x_ref[...] + gate * o
    o_ref[...] = _layer_norm(z, lng_ref[...], lnb_ref[...])


def _mixout(yf, yb, q, p_gm, x2, mods, mod_row, params, bd, slot, tm):
    ntok, d = x2.shape
    rw = yf.shape[-1]
    tok = lambda w: pl.BlockSpec((tm, w), lambda i: (i, 0))
    const = lambda a: pl.BlockSpec(a.shape, lambda i: (0,) * a.ndim, pipeline_mode=pl.Buffered(1))
    plist = [params[k] for k in ("r_k", "gn_g", "gn_b", "gm_ln_g", "gm_ln_b", "gm_ws", "gm_bs_full", "w_out",
                                 "ln_g", "ln_b")] + [bd]
    return pl.pallas_call(
        functools.partial(_mixout_kernel, slot=slot),
        grid=(ntok // tm,),
        in_specs=[tok(rw)] * 7 + [tok(p_gm.shape[1]), tok(d),
                                  pl.BlockSpec((1, N_MOD, d), lambda i: (mod_row(i), 0, 0))]
                 + [const(a) for a in plist],
        out_specs=tok(d),
        out_shape=jax.ShapeDtypeStruct((ntok, d), F32),
        compiler_params=_cparams(("parallel",)),
        name="mixout",
    )(yf, yb, q["r"], q["v"], q["g"], q["kd0"], q["kd1"], p_gm, x2, mods, *plist)


def kernel(x, c, ctx, c_ctx, w_ada, b_ada, ln_g, ln_b, ffn_a_wi, ffn_a_wo, ffn_b_wi, ffn_b_wo, w_in, mu_shift,
           w0, w_up, a0, a_up, g_up, k_k, k_a, r_k, gn_g, gn_b, gm_ln_g, gm_ln_b, gm_ws, gm_bs, w_out):
    bsz, seq, d = x.shape
    ctx_len = ctx.shape[1]
    assert w_ada.shape[0] == DEPTH
    rw = w0.shape[-1]
    rwkv_in = mu_shift.shape[-1]
    w_lora, a_lora = w_up.shape[2], a_up.shape[2]
    assert w_lora + a_lora == 128 and g_up.shape[1] == 128 and rwkv_in == 3 * rw + 256
    assert seq % GRID_W == 0 and seq % SCAN_CHUNK == 0 and ctx_len % SCAN_CHUNK == 0 and seq % CHUNK == 0
    i = 0

    n_rows = -(-(bsz + 1) // 8) * 8
    cc = jnp.zeros((n_rows, d), F32).at[:bsz].set(c).at[bsz].set(c_ctx)
    mods = _ada(cc, w_ada[i], b_ada[i]).reshape(n_rows, N_MOD, d)

    x2 = x.reshape(bsz * seq, d)
    c2 = ctx.reshape(bsz * ctx_len, d)
    tm_fx = _pick(seq, FFN_TM)
    tm_fc = _pick(bsz * ctx_len, FFN_TM)
    row_x = lambda tm: (lambda t: t // (seq // tm))
    row_c = lambda t: bsz

    bf = lambda a: a.astype(BF16)
    wi_a, wo_a = _ffn_weights(ffn_a_wi[i], ffn_a_wo[i])
    x1 = _ffn(x2, mods, row_x(tm_fx), wi_a, wo_a, ln_g[i, 0], ln_b[i, 0], 0, tm_fx)
    c1 = _ffn(c2, mods, row_c, wi_a, wo_a, ln_g[i, 0], ln_b[i, 0], 0, tm_fc)

    w_rw, w_gm = bf(w_in[i, :, :rwkv_in]), bf(w_in[i, :, rwkv_in:])
    p_rw_x = _inproj(x1, mods, row_x(tm_fx), w_rw, 1, tm_fx, rwkv_in // 2)
    p_rw_c = _inproj(c1, mods, row_c, w_rw, 1, tm_fc, rwkv_in // 2)

    zpad = lambda a, lo, hi: jnp.pad(a, ((0, 0), (lo, hi), (0, 0)))
    lanes = jnp.arange(GROUP_LANES) // RWKV_HEAD
    bd = (lanes[:, None] == lanes[None, :]).astype(BF16)
    prm = dict(mu=mu_shift[i].reshape(1, rwkv_in), w0=w0[i], w_up=bf(zpad(w_up[i], 0, a_lora)), a0=a0[i],
               a_up=bf(zpad(a_up[i], w_lora, 0)), g_up=bf(g_up[i]), k_k=k_k[i].reshape(1, rw),
               k_a=k_a[i].reshape(1, rw))
    tm_p = _pick(seq, 256)
    q_x = _prep(p_rw_x, prm, bd, seq, tm_p, gm=(x1, mods, row_x(tm_p), w_gm, 1))
    q_c = _prep(p_rw_c, prm, bd, ctx_len, ctx_len)
    p_gm_x = q_x.pop("p_gm")
    q_x3 = {k: a.reshape(bsz, seq, rw) for k, a in q_x.items()}
    q_c3 = {k: a.reshape(bsz, ctx_len, rw) for k, a in q_c.items()}

    ng = rw // GROUP_LANES
    s_zero = jnp.zeros((bsz, 2, ng, GROUP_LANES, GROUP_LANES), F32)
    s_ctx = _scan(q_c3, s_zero, with_y=True, n_sub=SCAN_SUB if ctx_len % (SCAN_CHUNK * SCAN_SUB) == 0 else 1)[-1]
    yf, yb, _ = _scan(q_x3, s_ctx, with_y=True, n_sub=SCAN_SUB)

    out_prm = dict(r_k=r_k[i].reshape(1, rw), gn_g=gn_g[i].reshape(1, rw), gn_b=gn_b[i].reshape(1, rw),
                   gm_ln_g=gm_ln_g[i].reshape(1, rw), gm_ln_b=gm_ln_b[i].reshape(1, rw), gm_ws=bf(gm_ws[i]),
                   gm_bs_full=jnp.repeat(gm_bs[i].T, GMLP_GROUP, axis=1), w_out=bf(w_out[i]),
                   ln_g=ln_g[i, 1].reshape(1, d), ln_b=ln_b[i, 1].reshape(1, d))
    tm_o = _pick(seq, 512)
    x2b = _mixout(yf.reshape(bsz * seq, rw), yb.reshape(bsz * seq, rw), q_x, p_gm_x, x1, mods, row_x(tm_o),
                  out_prm, bd, 1, tm_o)

    wi_b, wo_b = _ffn_weights(ffn_b_wi[i], ffn_b_wo[i])
    x3 = _ffn(x2b, mods, row_x(tm_fx), wi_b, wo_b, ln_g[i, 2], ln_b[i, 2], 2, tm_fx)
    return x3.reshape(bsz, seq, d)
```
